```python
import math, functools
import jax, jax.numpy as jnp
from jax import lax
import numpy as np

D_MODEL = 1024
BATCH = 8
SEQ = 2048
DEPTH = 2
DEC_BATCH = 32
DEC_SEQ = 4
PAST_LEN = 16384
PAGE_SIZE = 128

DH_A = 64
W_A = D_MODEL // 2
H_A = W_A // DH_A
SB_BIAS_INIT = -7.0

DK_B = 128
DV_B = 128
H_B = (D_MODEL // 2) // DK_B
W_BK = H_B * DK_B
W_BV = H_B * DV_B

D_FF = ((8 * D_MODEL // 3 + 127) // 128) * 128
CONV_W = 3

Q_BLOCK = 128
HGRN_CHUNK = 64
EPS = 1e-6
N_IN = 3 * W_A + 2 * W_BK + 2 * W_BV + 2 * D_MODEL

kernel_name = "stickbreak_hgrn2_convglu_adaln_step"


def _rmsnorm(x, gain):
    xf = x.astype(jnp.float32)
    y = xf * lax.rsqrt(jnp.mean(xf * xf, axis=-1, keepdims=True) + EPS)
    return (y * gain.astype(jnp.float32)).astype(x.dtype)


def _stick_breaking(z, mask):
    z = z.astype(jnp.float32)
    log_keep = jnp.where(mask, jax.nn.log_sigmoid(-z), 0.0)
    log_after = lax.cumsum(log_keep, axis=z.ndim - 1, reverse=True) - log_keep
    return jnp.where(mask, jnp.exp(jax.nn.log_sigmoid(z) + log_after), 0.0)


def _sb_attention_prompt(q, k, v, bias):
    b, t, h, d = q.shape
    nb = t // Q_BLOCK
    qb = q.reshape(b, nb, Q_BLOCK, h, d).transpose(1, 0, 2, 3, 4)
    kpos = jnp.arange(t)
    vf = v.astype(jnp.float32)
    bias_f = bias.astype(jnp.float32)[None, :, None, None]

    def one_block(args):
        q_blk, start = args
        qpos = start + jnp.arange(Q_BLOCK)
        z = jnp.einsum('bqhd,bkhd->bhqk', q_blk, k).astype(jnp.float32) * (d ** -0.5) + bias_f
        a = _stick_breaking(z, kpos[None, :] < qpos[:, None])
        return jnp.einsum('bhqk,bkhd->bqhd', a, vf)

    out = lax.map(one_block, (qb, jnp.arange(nb) * Q_BLOCK))
    return out.transpose(1, 0, 2, 3, 4).reshape(b, t, h, d).astype(q.dtype)


def _sb_attention_sample(q, k_new, v_new, bias, k_past, v_past):
    t, d = q.shape[1], q.shape[-1]
    p = k_past.shape[1]
    z = jnp.concatenate([jnp.einsum('bqhd,bkhd->bhqk', q, k_past),
                         jnp.einsum('bqhd,bkhd->bhqk', q, k_new)], axis=-1).astype(jnp.float32) * (d ** -0.5)
    z = z + bias.astype(jnp.float32)[None, :, None, None]
    pos = jnp.arange(t)
    mask = jnp.concatenate([jnp.ones((t, p), dtype=bool), pos[None, :] < pos[:, None]], axis=-1)
    a = _stick_breaking(z, mask)
    out = (jnp.einsum('bhqk,bkhd->bqhd', a[..., :p], v_past.astype(jnp.float32))
           + jnp.einsum('bhqk,bkhd->bqhd', a[..., p:], v_new.astype(jnp.float32)))
    return out.astype(q.dtype)


def _hgrn2(q, k, log_f, v, s0):
    b, t, h, dk = q.shape
    dv = v.shape[-1]
    c = math.gcd(t, HGRN_CHUNK)
    n = t // c

    def to_chunks(a):
        return a.reshape(b, n, c, h, a.shape[-1]).transpose(1, 0, 2, 3, 4)

    causal = jnp.tril(jnp.ones((c, c), dtype=bool))

    def step(s, inp):
        qc, kc, gc, vc = inp
        cum = jnp.cumsum(gc, axis=1)
        o = jnp.einsum('bchk,bhkv->bchv', qc * jnp.exp(cum), s)
        rel = jnp.where(causal[None, :, :, None, None], cum[:, :, None] - cum[:, None, :], -jnp.inf)
        att = jnp.einsum('bthk,btshk,bshk->bhts', qc, jnp.exp(rel), kc)
        o = o + jnp.einsum('bhts,bshv->bthv', att, vc)
        last = cum[:, -1]
        s = (jnp.exp(last)[..., None] * s
             + jnp.einsum('bshk,bshv->bhkv', kc * jnp.exp(last[:, None] - cum), vc))
        return s, o

    s_fin, o = lax.scan(step, s0.astype(jnp.float32),
                        (to_chunks(q), to_chunks(k), to_chunks(log_f), to_chunks(v)))
    return o.transpose(1, 0, 2, 3, 4).reshape(b, t, h, dv), s_fin


def _layer(x, c, p, lb, attend, s0, conv_prev):
    dt = x.dtype
    bsz, t, _ = x.shape
    mod = jnp.dot(jax.nn.silu(c), p['w_ada']) + p['b_ada']
    sh1, sc1, gt1, sh2, sc2, gt2 = [m[:, None, :] for m in jnp.split(mod, 6, axis=-1)]

    h = _rmsnorm(x, p['g_norm1']) * (1 + sc1) + sh1
    proj = jnp.einsum('btd,dn->btn', h, p['w_in'])
    sizes = [W_A, W_A, W_A, W_BK, W_BK, W_BV, W_BV, D_MODEL, D_MODEL]
    q_a, k_a, v_a, q_b, f_b, i_b, g_b, r_a, r_b = jnp.split(proj, np.cumsum(sizes)[:-1].tolist(), axis=-1)

    q_a = _rmsnorm(q_a.reshape(bsz, t, H_A, DH_A), p['q_gain'])
    k_a = _rmsnorm(k_a.reshape(bsz, t, H_A, DH_A), p['k_gain'])
    v_a = v_a.reshape(bsz, t, H_A, DH_A)
    o_a = attend(q_a, k_a, v_a, p['sb_bias']).reshape(bsz, t, W_A).astype(dt)

    zf = f_b.reshape(bsz, t, H_B, DK_B).astype(jnp.float32)
    lbh = lb.reshape(H_B, DK_B)
    log_f = jnp.log(lbh + (1.0 - lbh) * jax.nn.sigmoid(zf))
    k_b = (1.0 - lbh) * jax.nn.sigmoid(-zf)
    q_bh = jax.nn.silu(q_b.reshape(bsz, t, H_B, DK_B).astype(jnp.float32))
    v_bh = i_b.reshape(bsz, t, H_B, DV_B).astype(jnp.float32)
    o_b, s_new = _hgrn2(q_bh, k_b, log_f, v_bh, s0)
    o_b = _rmsnorm(o_b, p['hgrn_gain']) * jax.nn.silu(g_b.reshape(bsz, t, H_B, DV_B).astype(jnp.float32))
    o_b = o_b.reshape(bsz, t, W_BV).astype(dt)

    merged = (jax.nn.sigmoid(r_a) * jnp.einsum('btw,wd->btd', o_a, p['w_pa'])
              + jax.nn.sigmoid(r_b) * jnp.einsum('btw,wd->btd', o_b, p['w_pb']))
    x = x + gt1 * jnp.einsum('btd,de->bte', merged, p['w_o'])

    h = _rmsnorm(x, p['g_norm2']) * (1 + sc2) + sh2
    up = jnp.einsum('btd,df->btf', h, p['w_up'])
    u, v = jnp.split(up, 2, axis=-1)
    u_ext = jnp.concatenate([conv_prev.astype(u.dtype), u], axis=1)
    conv = p['conv_b'] + u_ext[:, 0:t] * p['conv_w'][0]
    for j in range(1, CONV_W):
        conv = conv + u_ext[:, j:j + t] * p['conv_w'][j]
    x = x + gt2 * jnp.einsum('btf,fd->btd', jax.nn.gelu(conv) * v, p['w_down'])
    return x, k_a, v_a, s_new, u_ext[:, -(CONV_W - 1):]


def setup_inputs(seed: int = 0) -> dict:
    key = jax.random.key(seed)
    ks = jax.random.split(key, 32)
    f32 = jnp.float32
    n_pages = PAST_LEN // PAGE_SIZE
    used = DEC_BATCH * n_pages
    n_pool = used + max(1, used // 4)
    nrm = lambda k, shape, s=1.0: (jax.random.normal(k, shape, f32) * s).astype(f32)
    page_table = jax.random.permutation(ks[0], n_pool)[:used].reshape(DEC_BATCH, n_pages).astype(jnp.int32)
    return {
        'x_prompt': nrm(ks[1], (BATCH, SEQ, D_MODEL)),
        'x_sample': nrm(ks[2], (DEC_BATCH, DEC_SEQ, D_MODEL)),
        'cache_k': nrm(ks[3], (DEPTH, n_pool, PAGE_SIZE, H_A, DH_A)),
        'cache_v': nrm(ks[4], (DEPTH, n_pool, PAGE_SIZE, H_A, DH_A)),
        'state_hgrn': nrm(ks[5], (DEPTH, DEC_BATCH, H_B, DK_B, DV_B), 0.5),
        'state_conv': nrm(ks[6], (DEPTH, DEC_BATCH, CONV_W - 1, D_FF)),
        'page_table': page_table,
        'c_prompt': nrm(ks[7], (BATCH, D_MODEL)),
        'c_sample': nrm(ks[8], (DEC_BATCH, D_MODEL)),
        'w_ada': nrm(ks[9], (DEPTH, D_MODEL, 6 * D_MODEL), 0.5 * D_MODEL ** -0.5),
        'b_ada': nrm(ks[10], (DEPTH, 6 * D_MODEL), 0.01),
        'g_norm1': 1.0 + nrm(ks[11], (DEPTH, D_MODEL), 0.01),
        'w_in': nrm(ks[12], (DEPTH, D_MODEL, N_IN), D_MODEL ** -0.5),
        'q_gain': 1.0 + nrm(ks[13], (DEPTH, DH_A), 0.01),
        'k_gain': 1.0 + nrm(ks[14], (DEPTH, DH_A), 0.01),
        'sb_bias': SB_BIAS_INIT + nrm(ks[25], (DEPTH, H_A), 0.5),
        'hgrn_lb_logits': nrm(ks[15], (DEPTH, W_BK), 0.5),
        'hgrn_gain': 1.0 + nrm(ks[16], (DEPTH, DV_B), 0.01),
        'w_pa': nrm(ks[17], (DEPTH, W_A, D_MODEL), W_A ** -0.5),
        'w_pb': nrm(ks[18], (DEPTH, W_BV, D_MODEL), W_BV ** -0.5),
        'w_o': nrm(ks[19], (DEPTH, D_MODEL, D_MODEL), D_MODEL ** -0.5),
        'g_norm2': 1.0 + nrm(ks[20], (DEPTH, D_MODEL), 0.01),
        'w_up': nrm(ks[21], (DEPTH, D_MODEL, 2 * D_FF), D_MODEL ** -0.5),
        'conv_w': nrm(ks[22], (DEPTH, CONV_W, D_FF), CONV_W ** -0.5),
        'conv_b': nrm(ks[23], (DEPTH, D_FF), 0.01),
        'w_down': nrm(ks[24], (DEPTH, D_FF, D_MODEL), D_FF ** -0.5),
    }


def reference(x_prompt, x_sample, cache_k, cache_v, state_hgrn, state_conv, page_table, c_prompt, c_sample,
              w_ada, b_ada, g_norm1, w_in, q_gain, k_gain, sb_bias, hgrn_lb_logits, hgrn_gain, w_pa, w_pb, w_o,
              g_norm2, w_up, conv_w, conv_b, w_down):
    lb_w = jax.nn.softmax(hgrn_lb_logits.astype(jnp.float32), axis=0)
    lb_all = jnp.cumsum(lb_w, axis=0) - lb_w[0]

    n_dec = x_sample.shape[0]
    n_pr = x_prompt.shape[0]
    yp, ys = x_prompt, x_sample
    kp_l, vp_l, ks_l, vs_l, hp_l, hs_l, cp_l, cs_l = [], [], [], [], [], [], [], []
    for l in range(DEPTH):
        p = {'w_ada': w_ada[l], 'b_ada': b_ada[l], 'g_norm1': g_norm1[l], 'w_in': w_in[l],
             'q_gain': q_gain[l], 'k_gain': k_gain[l], 'sb_bias': sb_bias[l], 'hgrn_gain': hgrn_gain[l],
             'w_pa': w_pa[l], 'w_pb': w_pb[l], 'w_o': w_o[l], 'g_norm2': g_norm2[l],
             'w_up': w_up[l], 'conv_w': conv_w[l], 'conv_b': conv_b[l], 'w_down': w_down[l]}

        s0_p = jnp.zeros((n_pr, H_B, DK_B, DV_B), jnp.float32)
        conv0_p = jnp.zeros((n_pr, CONV_W - 1, D_FF), x_prompt.dtype)
        yp, kp, vp, hp, cp = _layer(yp, c_prompt, p, lb_all[l], _sb_attention_prompt, s0_p, conv0_p)

        k_past = cache_k[l][page_table].reshape(n_dec, -1, H_A, DH_A)
        v_past = cache_v[l][page_table].reshape(n_dec, -1, H_A, DH_A)
        attend_s = functools.partial(_sb_attention_sample, k_past=k_past, v_past=v_past)
        ys, ksm, vsm, hs, cs = _layer(ys, c_sample, p, lb_all[l], attend_s, state_hgrn[l], state_conv[l])

        kp_l.append(kp); vp_l.append(vp); ks_l.append(ksm); vs_l.append(vsm)
        hp_l.append(hp); hs_l.append(hs); cp_l.append(cp); cs_l.append(cs)

    return (yp, ys, jnp.stack(kp_l), jnp.stack(vp_l), jnp.stack(ks_l), jnp.stack(vs_l),
            jnp.stack(hp_l), jnp.stack(hs_l), jnp.stack(cp_l), jnp.stack(cs_l))
```

```python
import functools

import numpy as np
import jax
import jax.numpy as jnp
from jax import lax
from jax.experimental import pallas as pl
from jax.experimental.pallas import tpu as pltpu

F32 = jnp.float32
BF16 = jnp.bfloat16

EPS = 1e-6
V7X_LANES = 128
V7X_VMEM_LIMIT_BYTES = 56 * 1024 * 1024

H_A = 8
DH_A = 64
W_A = H_A * DH_A
H_B = 4
DK_B = 128
W_B = H_B * DK_B
CONV_W = 3
HGRN_CHUNK = 64


def _cparams(sem):
    return pltpu.CompilerParams(dimension_semantics=sem, vmem_limit_bytes=V7X_VMEM_LIMIT_BYTES)


def _sigmoid(x):
    return 1.0 / (1.0 + jnp.exp(-x))


def _dot(a, b):
    return jnp.dot(a, b, preferred_element_type=F32)


def _dot_nt(a, b):
    return lax.dot_general(a, b, (((1,), (1,)), ((), ())), preferred_element_type=F32)


def _split2(x):
    hi = x.astype(BF16)
    lo = (x - hi.astype(F32)).astype(BF16)
    return hi, lo


def _ada_kernel(c_ref, w_ref, b_ref, o_ref):
    c = c_ref[...]
    s = c * _sigmoid(c)
    s_hi, s_lo = _split2(s)
    w_hi, w_lo = _split2(w_ref[...])
    acc = _dot(s_hi, w_hi) + _dot(s_hi, w_lo) + _dot(s_lo, w_hi)
    o_ref[...] = acc + b_ref[...]


def _ada(c_all, w_ada, b_ada):
    depth, d, n6 = w_ada.shape
    nc = c_all.shape[0]
    tn = 1536
    return pl.pallas_call(
        _ada_kernel,
        out_shape=jax.ShapeDtypeStruct((depth, nc, n6), F32),
        grid=(depth, n6 // tn),
        in_specs=[
            pl.BlockSpec((nc, d), lambda l, j: (0, 0)),
            pl.BlockSpec((None, d, tn), lambda l, j: (l, 0, j)),
            pl.BlockSpec((None, 1, tn), lambda l, j: (l, 0, j)),
        ],
        out_specs=pl.BlockSpec((None, nc, tn), lambda l, j: (l, 0, j)),
        compiler_params=_cparams(("parallel", "parallel")),
        name="ada_mod",
    )(c_all, w_ada, b_ada.reshape(depth, 1, n6))


IN_TN = 512


def _in_kernel(x_ref, g_ref, sc_ref, sh_ref, w_ref, bd_ref, qg_ref, kg_ref,
               qkv_ref, kv_ref, hg_ref, r_ref, h_scr):
    j = pl.program_id(1)

    @pl.when(j == 0)
    def _():
        x = x_ref[...]
        ms = jnp.mean(x * x, axis=-1, keepdims=True)
        y = x * lax.rsqrt(ms + EPS) * g_ref[...]
        h_scr[...] = (y * (1.0 + sc_ref[...]) + sh_ref[...]).astype(BF16)

    acc = _dot(h_scr[...], w_ref[...])

    def head_norm(t):
        sq_hi, sq_lo = _split2(t * t)
        ms = _dot(sq_hi, bd_ref[...]) + _dot(sq_lo, bd_ref[...])
        return t * lax.rsqrt(ms + EPS)

    @pl.when(j == 0)
    def _():
        qkv_ref[...] = (head_norm(acc) * qg_ref[...] * (DH_A ** -0.5)).astype(BF16)

    @pl.when(j == 1)
    def _():
        k = head_norm(acc) * kg_ref[...]
        kv_ref[...] = k
        qkv_ref[...] = k.astype(BF16)

    @pl.when(j == 2)
    def _():
        kv_ref[...] = acc
        qkv_ref[...] = acc.astype(BF16)

    @pl.when(jnp.logical_and(j >= 3, j <= 6))
    def _():
        hg_ref[...] = acc

    @pl.when(j >= 7)
    def _():
        r_ref[...] = _sigmoid(acc)


def _in_proj(x2, g, sc, sh, w_bf, bd, qg, kg, *, tm, tiles_per_group):
    rows, d = x2.shape
    n_in = w_bf.shape[1]
    nj = n_in // IN_TN
    rb = sc.shape[1]
    mod_spec = pl.BlockSpec((None, rb, d), lambda i, j: (i // tiles_per_group, 0, 0))
    return pl.pallas_call(
        _in_kernel,
        out_shape=(
            jax.ShapeDtypeStruct((rows, 3 * W_A), BF16),
            jax.ShapeDtypeStruct((2, rows, W_A), F32),
            jax.ShapeDtypeStruct((rows, 4 * W_B), F32),
            jax.ShapeDtypeStruct((rows, 2 * d), F32),
        ),
        grid=(rows // tm, nj),
        in_specs=[
            pl.BlockSpec((tm, d), lambda i, j: (i, 0)),
            pl.BlockSpec((1, d), lambda i, j: (0, 0)),
            mod_spec,
            mod_spec,
            pl.BlockSpec((d, IN_TN), lambda i, j: (0, j)),
            pl.BlockSpec((W_A, W_A), lambda i, j: (0, 0)),
            pl.BlockSpec((1, W_A), lambda i, j: (0, 0)),
            pl.BlockSpec((1, W_A), lambda i, j: (0, 0)),
        ],
        out_specs=(
            pl.BlockSpec((tm, IN_TN), lambda i, j: (i, jnp.minimum(j, 2))),
            pl.BlockSpec((None, tm, IN_TN), lambda i, j: (jnp.clip(j - 1, 0, 1), i, 0)),
            pl.BlockSpec((tm, IN_TN), lambda i, j: (i, jnp.clip(j - 3, 0, 3))),
            pl.BlockSpec((tm, IN_TN), lambda i, j: (i, jnp.clip(j - 7, 0, 3))),
        ),
        scratch_shapes=[pltpu.VMEM((tm, d), BF16)],
        compiler_params=_cparams(("parallel", "arbitrary")),
        name="in_proj",
    )(x2, g, sc, sh, w_bf, bd, qg, kg)


def _sb_block(z, c, uo, mask):
    n = z.shape[1]
    e = jnp.exp(-jnp.abs(z))
    log_keep = -(jnp.maximum(z, 0.0) + jnp.log(1.0 + e))
    lk = log_keep if mask is None else jnp.where(mask, log_keep, 0.0)
    lk_hi, lk_lo = _split2(lk)
    cs = _dot(lk_hi, uo) + _dot(lk_lo, uo)
    after = cs[:, :n]
    tot = cs[:, n:]
    c_full = c if n == V7X_LANES else jnp.concatenate([c] * (n // V7X_LANES), axis=1)
    a = jnp.exp(z + log_keep + after + c_full)
    if mask is not None:
        a = jnp.where(mask, a, 0.0)
    return a, tot


def _suffix_matrix(n):
    j = np.arange(n)[:, None]
    s = np.arange(n)[None, :]
    u = (j > s).astype(np.float32)
    return jnp.asarray(np.concatenate([u, np.ones((n, V7X_LANES), np.float32)], axis=1), dtype=BF16)


ATT_TQ = 256
ATT_TK = 256


def _attn_prompt_kernel(bias_ref, q_ref, k_ref, v_ref, uo_ref, o_ref, km_scr, acc_scr, c_scr):
    hp = pl.program_id(1)
    qi = pl.program_id(2)
    tq, tk = ATT_TQ, ATT_TK
    lane = lax.broadcasted_iota(jnp.int32, (1, V7X_LANES), 1)
    first_head = lane < DH_A

    @pl.when(qi == 0)
    def _():
        k = k_ref[...]
        km_scr[0] = jnp.where(first_head, k, jnp.zeros_like(k))
        km_scr[1] = jnp.where(first_head, jnp.zeros_like(k), k)

    q = q_ref[...]
    uo = uo_ref[...]
    row = lax.broadcasted_iota(jnp.int32, (tq, tk), 0)
    col = lax.broadcasted_iota(jnp.int32, (tq, tk), 1)
    causal = col < row

    for hh in range(2):
        bias = bias_ref[2 * hp + hh]

        def step(kb, mask, hh=hh, bias=bias):
            start = pl.multiple_of(kb * tk, tk)
            kblk = km_scr[hh, pl.ds(start, tk), :]
            vblk = v_ref[pl.ds(start, tk), :]
            z = _dot_nt(q, kblk) + bias
            a, tot = _sb_block(z, c_scr[hh], uo, mask)
            acc_scr[hh] = acc_scr[hh] + _dot(a.astype(BF16), vblk)
            c_scr[hh] = c_scr[hh] + tot

        acc_scr[hh] = jnp.zeros((tq, V7X_LANES), F32)
        c_scr[hh] = jnp.zeros((tq, V7X_LANES), F32)
        step(qi, causal)

        def body(i, carry):
            step(qi - 1 - i, None)
            return carry

        lax.fori_loop(0, qi, body, 0)

    o_ref[...] = jnp.where(first_head, acc_scr[0], acc_scr[1]).astype(o_ref.dtype)


def _attn_prompt(qkv, sb_bias, uo, *, batch, seq):
    rows = qkv.shape[0]
    nq = seq // ATT_TQ
    npair = H_A // 2
    return pl.pallas_call(
        _attn_prompt_kernel,
        out_shape=jax.ShapeDtypeStruct((rows, W_A), BF16),
        grid=(batch, npair, nq),
        in_specs=[
            pl.BlockSpec(memory_space=pltpu.SMEM),
            pl.BlockSpec((ATT_TQ, V7X_LANES), lambda b, hp, qi: (b * nq + qi, hp)),
            pl.BlockSpec((seq, V7X_LANES), lambda b, hp, qi: (b, npair + hp)),
            pl.BlockSpec((seq, V7X_LANES), lambda b, hp, qi: (b, 2 * npair + hp)),
            pl.BlockSpec((ATT_TK, ATT_TK + V7X_LANES), lambda b, hp, qi: (0, 0)),
        ],
        out_specs=pl.BlockSpec((ATT_TQ, V7X_LANES), lambda b, hp, qi: (b * nq + qi, hp)),
        scratch_shapes=[
            pltpu.VMEM((2, seq, V7X_LANES), BF16),
            pltpu.VMEM((2, ATT_TQ, V7X_LANES), F32),
            pltpu.VMEM((2, ATT_TQ, V7X_LANES), F32),
        ],
        compiler_params=_cparams(("parallel", "parallel", "arbitrary")),
        name="attn_prompt",
    )(sb_bias, qkv, qkv, qkv, uo)


PAGES_PER_STEP = 8


def _attn_sample_kernel(pt_ref, qbd_ref, bias_ref, knew_ref, vnew_ref, uo_ref, *rest):
    npg = PAGES_PER_STEP
    k_refs = rest[:npg]
    v_refs = rest[npg:2 * npg]
    o_ref = rest[2 * npg]
    acc_scr, c_scr = rest[2 * npg + 1:]
    p = pl.program_id(1)
    nrow = qbd_ref.shape[0]
    dec_seq = nrow // H_A
    page = uo_ref.shape[0]
    qbd = qbd_ref[...]
    bias = bias_ref[...]
    uo = uo_ref[...]

    def step(kpage, vpage, mask):
        z = _dot_nt(qbd, kpage.astype(BF16)) + bias
        a, tot = _sb_block(z, c_scr[...], uo, mask)
        acc_scr[...] = acc_scr[...] + _dot(a.astype(BF16), vpage.astype(BF16))
        c_scr[...] = c_scr[...] + tot

    @pl.when(p == 0)
    def _():
        acc_scr[...] = jnp.zeros_like(acc_scr)
        c_scr[...] = jnp.zeros_like(c_scr)
        pad = jnp.zeros((page - knew_ref.shape[0], W_A), F32)
        t = lax.broadcasted_iota(jnp.int32, (nrow, page), 0) >> 3
        s = lax.broadcasted_iota(jnp.int32, (nrow, page), 1)
        step(jnp.concatenate([knew_ref[...], pad], axis=0),
             jnp.concatenate([vnew_ref[...], pad], axis=0), s < t)

    for i in range(npg):
        step(k_refs[i][...], v_refs[i][...], None)

    @pl.when(p == pl.num_programs(1) - 1)
    def _():
        r = lax.broadcasted_iota(jnp.int32, (nrow, W_A), 0)
        l = lax.broadcasted_iota(jnp.int32, (nrow, W_A), 1)
        own = (r & (H_A - 1)) == (l >> 6)
        om = jnp.where(own, acc_scr[...], 0.0)
        o_ref[...] = jnp.sum(om.reshape(dec_seq, H_A, W_A), axis=1)


def _attn_sample(page_table, qbd, bias_rows, knew, vnew, uo, cache_k, cache_v, *, layer):
    nb, nrow, _ = qbd.shape
    n_pages = page_table.shape[1]
    page = cache_k.shape[2]
    nsteps = n_pages // PAGES_PER_STEP
    dec_seq = nrow // H_A

    def page_spec(i):
        def imap(b, p, pt):
            return (layer, pt[b, n_pages - 1 - (p * PAGES_PER_STEP + i)], 0, 0)
        return pl.BlockSpec((None, None, page, W_A), imap)

    grid_spec = pltpu.PrefetchScalarGridSpec(
        num_scalar_prefetch=1,
        grid=(nb, nsteps),
        in_specs=[
            pl.BlockSpec((None, nrow, W_A), lambda b, p, pt: (b, 0, 0)),
            pl.BlockSpec((nrow, V7X_LANES), lambda b, p, pt: (0, 0)),
            pl.BlockSpec((None, 8, W_A), lambda b, p, pt: (b, 0, 0)),
            pl.BlockSpec((None, 8, W_A), lambda b, p, pt: (b, 0, 0)),
            pl.BlockSpec((page, page + V7X_LANES), lambda b, p, pt: (0, 0)),
        ] + [page_spec(i) for i in range(PAGES_PER_STEP)] + [page_spec(i) for i in range(PAGES_PER_STEP)],
        out_specs=pl.BlockSpec((None, dec_seq, W_A), lambda b, p, pt: (b, 0, 0)),
        scratch_shapes=[
            pltpu.VMEM((nrow, W_A), F32),
            pltpu.VMEM((nrow, V7X_LANES), F32),
        ],
    )
    return pl.pallas_call(
        _attn_sample_kernel,
        out_shape=jax.ShapeDtypeStruct((nb, dec_seq, W_A), F32),
        grid_spec=grid_spec,
        compiler_params=_cparams(("parallel", "arbitrary")),
        name="attn_sample",
    )(page_table, qbd, bias_rows, knew, vnew, uo,
      *([cache_k] * PAGES_PER_STEP), *([cache_v] * PAGES_PER_STEP))


def _seg_bcast(x, n, off):
    rows, lanes = x.shape
    if n >= 8:
        parts = [jnp.broadcast_to(x[b * n + off:b * n + off + 1, :], (n, lanes)) for b in range(rows // n)]
        return jnp.concatenate(parts, axis=0)
    sub = lax.broadcasted_iota(jnp.int32, (8, lanes), 0)
    parts = []
    for g in range(rows // 8):
        acc = jnp.broadcast_to(x[g * 8 + off:g * 8 + off + 1, :], (8, lanes))
        for sb in range(1, 8 // n):
            r = g * 8 + sb * n + off
            acc = jnp.where(sub >= sb * n, jnp.broadcast_to(x[r:r + 1, :], (8, lanes)), acc)
        parts.append(acc)
    return jnp.concatenate(parts, axis=0)


def _hgrn_levels(chunk):
    out = []
    n = chunk
    while n >= 2:
        out.append(n)
        n //= 2
    return out


def _hgrn_masks(chunk, groups):
    rows = chunk * groups
    t = np.arange(rows)
    m = [(t[:, None] // n == t[None, :] // n) for n in _hgrn_levels(chunk)]
    m.append(t[:, None] == t[None, :])
    return jnp.asarray(np.stack(m).astype(np.float32))


def _hgrn_kernel(lbl_ref, gain_ref, masks_ref, tri_ref, hg_ref, s0_ref, ob_ref, sout_ref, s_scr,
                 *, layer, chunk, valid, nseq, nchunks):
    j = pl.program_id(1)
    groups = nseq * H_B
    rows = groups * chunk
    levels = _hgrn_levels(chunk)

    @pl.when(j == 0)
    def _():
        s_scr[...] = s0_ref[...].reshape(groups, DK_B, DK_B)

    lg = lbl_ref[...]
    ex = jnp.exp(lg - jnp.max(lg, axis=0, keepdims=True))
    wts = ex / jnp.sum(ex, axis=0, keepdims=True)
    lb_row = jnp.sum(wts[:layer + 1], axis=0, keepdims=True) - wts[0:1]

    def stack(get):
        return jnp.concatenate([get(s, h) for s in range(nseq) for h in range(H_B)], axis=0)

    lbs = stack(lambda s, h: jnp.broadcast_to(lb_row[:, h * DK_B:(h + 1) * DK_B], (chunk, DK_B)))
    tloc = lax.broadcasted_iota(jnp.int32, (rows, DK_B), 0) & (chunk - 1)
    tri = tri_ref[...]
    gain = gain_ref[...]

    for ci in range(nchunks):
        r0 = ci * chunk

        def seg(col, r0=r0):
            return stack(lambda s, h: hg_ref[s, r0:r0 + chunk, (col * H_B + h) * DK_B:(col * H_B + h + 1) * DK_B])

        zq, zf, vi, zg = seg(0), seg(1), seg(2), seg(3)
        logf = jnp.log(lbs + (1.0 - lbs) * _sigmoid(zf))
        kk = (1.0 - lbs) * _sigmoid(-zf)
        qs = zq * _sigmoid(zq)
        if valid < chunk:
            ok = tloc < valid
            logf = jnp.where(ok, logf, 0.0)
            kk = jnp.where(ok, kk, 0.0)
            qs = jnp.where(ok, qs, 0.0)

        l_hi = logf.astype(BF16)
        rem = logf - l_hi.astype(F32)
        l_mid = rem.astype(BF16)
        l_lo = (rem - l_mid.astype(F32)).astype(BF16)
        cum = _dot(tri, l_hi) + _dot(tri, l_mid) + _dot(tri, l_lo)

        att = _dot_nt(qs.astype(BF16), kk.astype(BF16)) * masks_ref[len(levels)]
        for li, n in enumerate(levels):
            ref_row = _seg_bcast(cum, n, n // 2 - 1)
            e = jnp.exp(-jnp.abs(cum - ref_row))
            upper = (tloc & (n - 1)) >= (n // 2)
            qn = jnp.where(upper, qs * e, 0.0).astype(BF16)
            kn = jnp.where(upper, 0.0, kk * e).astype(BF16)
            att = att + _dot_nt(qn, kn) * masks_ref[li]

        o = _dot(att.astype(BF16), vi.astype(BF16))
        qe = qs * jnp.exp(cum)
        o = o + jnp.concatenate(
            [_dot(qe[g * chunk:(g + 1) * chunk].astype(BF16), s_scr[g].astype(BF16)) for g in range(groups)],
            axis=0)

        on = o * lax.rsqrt(jnp.mean(o * o, axis=-1, keepdims=True) + EPS) * gain
        og = on * (zg * _sigmoid(zg))
        for s in range(nseq):
            ob_ref[s, r0:r0 + chunk, :] = jnp.concatenate(
                [og[(s * H_B + h) * chunk:(s * H_B + h + 1) * chunk] for h in range(H_B)], axis=1)

        last = _seg_bcast(cum, chunk, chunk - 1)
        kd_t = (kk * jnp.exp(last - cum)).T
        dec_t = jnp.exp(last).T
        colg = lax.broadcasted_iota(jnp.int32, (DK_B, rows), 1) >> (chunk.bit_length() - 1)
        vb = vi.astype(BF16)
        for g in range(groups):
            kd_g = jnp.where(colg == g, kd_t, 0.0).astype(BF16)
            s_scr[g] = dec_t[:, g * chunk:g * chunk + 1] * s_scr[g] + _dot(kd_g, vb)

    @pl.when(j == pl.num_programs(1) - 1)
    def _():
        sout_ref[...] = s_scr[...].reshape(nseq, H_B, DK_B, DK_B)


def _hgrn(hg3, s0, lb_logits, gain, *, layer, chunk, valid, nseq, nchunks):
    nb, tp, _ = hg3.shape
    rb = chunk * nchunks
    groups = nseq * H_B
    rows = groups * chunk
    masks = _hgrn_masks(chunk, groups)
    t = np.arange(rows)
    tri = jnp.asarray(((t[:, None] // chunk == t[None, :] // chunk) & (t[None, :] <= t[:, None])).astype(np.float32),
                      dtype=BF16)
    kern = functools.partial(_hgrn_kernel, layer=layer, chunk=chunk, valid=valid, nseq=nseq, nchunks=nchunks)
    return pl.pallas_call(
        kern,
        out_shape=(
            jax.ShapeDtypeStruct((nb, tp, W_B), F32),
            jax.ShapeDtypeStruct((nb, H_B, DK_B, DK_B), F32),
        ),
        grid=(nb // nseq, tp // rb),
        in_specs=[
            pl.BlockSpec(lb_logits.shape, lambda b, j: (0, 0)),
            pl.BlockSpec((1, DK_B), lambda b, j: (0, 0)),
            pl.BlockSpec(masks.shape, lambda b, j: (0, 0, 0)),
            pl.BlockSpec((rows, rows), lambda b, j: (0, 0)),
            pl.BlockSpec((nseq, rb, 4 * W_B), lambda b, j: (b, j, 0)),
            pl.BlockSpec((nseq, H_B, DK_B, DK_B), lambda b, j: (b, 0, 0, 0)),
        ],
        out_specs=(
            pl.BlockSpec((nseq, rb, W_B), lambda b, j: (b, j, 0)),
            pl.BlockSpec((nseq, H_B, DK_B, DK_B), lambda b, j: (b, 0, 0, 0)),
        ),
        scratch_shapes=[pltpu.VMEM((groups, DK_B, DK_B), F32)],
        compiler_params=_cparams(("parallel", "arbitrary")),
        name="hgrn",
    )(lb_logits, gain, masks, tri, hg3, s0)


def _mix_kernel(oa_ref, ob_ref, r_ref, x_ref, gt_ref, wpa_ref, wpb_ref, wo_ref, out_ref):
    d = x_ref.shape[1]
    pa = _dot(oa_ref[...], wpa_ref[...])
    pb = _dot(ob_ref[...].astype(BF16), wpb_ref[...])
    merged = r_ref[:, :d] * pa + r_ref[:, d:] * pb
    y = _dot(merged.astype(BF16), wo_ref[...])
    out_ref[...] = x_ref[...] + gt_ref[...] * y


def _mix(oa, ob, r, x2, gt, wpa, wpb, wo, *, tm, tiles_per_group):
    rows, d = x2.shape
    rb = gt.shape[1]
    return pl.pallas_call(
        _mix_kernel,
        out_shape=jax.ShapeDtypeStruct((rows, d), F32),
        grid=(rows // tm,),
        in_specs=[
            pl.BlockSpec((tm, W_A), lambda i: (i, 0)),
            pl.BlockSpec((tm, W_B), lambda i: (i, 0)),
            pl.BlockSpec((tm, 2 * d), lambda i: (i, 0)),
            pl.BlockSpec((tm, d), lambda i: (i, 0)),
            pl.BlockSpec((None, rb, d), lambda i: (i // tiles_per_group, 0, 0)),
            pl.BlockSpec((W_A, d), lambda i: (0, 0)),
            pl.BlockSpec((W_B, d), lambda i: (0, 0)),
            pl.BlockSpec((d, d), lambda i: (0, 0)),
        ],
        out_specs=pl.BlockSpec((tm, d), lambda i: (i, 0)),
        compiler_params=_cparams(("parallel",)),
        name="mix",
    )(oa, ob, r, x2, gt, wpa, wpb, wo)


FFN_TC = 256


def _gelu_tanh(x):
    return 0.5 * x * (1.0 + jnp.tanh(0.7978845608028654 * (x + 0.044715 * x * x * x)))


def _ffn_kernel(x_ref, g_ref, sc_ref, sh_ref, gt_ref, wu_ref, wv_ref, cw_ref, cb_ref, wd_ref, e1_ref, e2_ref,
                out_ref, ut_ref, h_scr, acc_scr, carry_scr, *, seq, tiles_per_seq):
    i = pl.program_id(0)
    c = pl.program_id(1)
    tm = x_ref.shape[0]
    tc = wu_ref.shape[1]

    @pl.when(c == 0)
    def _():
        x = x_ref[...]
        ms = jnp.mean(x * x, axis=-1, keepdims=True)
        y = x * lax.rsqrt(ms + EPS) * g_ref[...]
        h_scr[...] = (y * (1.0 + sc_ref[...]) + sh_ref[...]).astype(BF16)
        acc_scr[...] = jnp.zeros_like(acc_scr)

    h = h_scr[...]
    u = _dot(h, wu_ref[...])
    v = _dot(h, wv_ref[...])
    row = lax.broadcasted_iota(jnp.int32, (tm, tc), 0)
    col0 = pl.multiple_of(c * tc, tc)
    if tiles_per_seq >= 1:
        @pl.when((i % tiles_per_seq) == 0)
        def _():
            carry_scr[:, pl.ds(col0, tc)] = e1_ref[...]

        prev = carry_scr[:, pl.ds(col0, tc)]
        m1 = jnp.where(row >= 1, pltpu.roll(u, 1, 0), jnp.broadcast_to(prev[1:2], (tm, tc)))
        m2 = jnp.where(row >= 2, pltpu.roll(u, 2, 0),
                       jnp.where(row == 1, jnp.broadcast_to(prev[1:2], (tm, tc)),
                                 jnp.broadcast_to(prev[0:1], (tm, tc))))
        carry_scr[:, pl.ds(col0, tc)] = u[tm - 2:tm, :]
    else:
        t = row & (seq - 1)
        m1 = jnp.where(t >= 1, pltpu.roll(u, 1, 0), e1_ref[...])
        m2 = jnp.where(t >= 2, pltpu.roll(u, 2, 0), e2_ref[...])
    conv = cb_ref[...] + m2 * cw_ref[0:1, :] + m1 * cw_ref[1:2, :] + u * cw_ref[2:3, :]
    act = (_gelu_tanh(conv) * v).astype(BF16)
    acc_scr[...] = acc_scr[...] + _dot(act, wd_ref[...])
    ut_ref[...] = u[tm - ut_ref.shape[0]:, :]

    @pl.when(c == pl.num_programs(1) - 1)
    def _():
        out_ref[...] = x_ref[...] + gt_ref[...] * acc_scr[...]


def _ffn(x2, g, sc, sh, gt, wup, cw, cb, wd, e1, e2, *, tm, tiles_per_group, seq, ut_rows):
    rows, d = x2.shape
    d_ff = wd.shape[0]
    nc = d_ff // FFN_TC
    rb = sc.shape[1]
    tiles_per_seq = seq // tm
    mod_spec = pl.BlockSpec((None, rb, d), lambda i, c: (i // tiles_per_group, 0, 0))
    if tiles_per_seq >= 1:
        e_spec = pl.BlockSpec((None, 2, FFN_TC), lambda i, c: (i // tiles_per_seq, 0, c))
    else:
        e_spec = pl.BlockSpec((tm, FFN_TC), lambda i, c: (i, c))
    kern = functools.partial(_ffn_kernel, seq=seq, tiles_per_seq=tiles_per_seq)
    return pl.pallas_call(
        kern,
        out_shape=(
            jax.ShapeDtypeStruct((rows, d), F32),
            jax.ShapeDtypeStruct((rows // tm, ut_rows, d_ff), F32),
        ),
        grid=(rows // tm, nc),
        in_specs=[
            pl.BlockSpec((tm, d), lambda i, c: (i, 0)),
            pl.BlockSpec((1, d), lambda i, c: (0, 0)),
            mod_spec, mod_spec, mod_spec,
            pl.BlockSpec((d, FFN_TC), lambda i, c: (0, c)),
            pl.BlockSpec((d, FFN_TC), lambda i, c: (0, nc + c)),
            pl.BlockSpec((CONV_W, FFN_TC), lambda i, c: (0, c)),
            pl.BlockSpec((1, FFN_TC), lambda i, c: (0, c)),
            pl.BlockSpec((FFN_TC, d), lambda i, c: (c, 0)),
            e_spec, e_spec,
        ],
        out_specs=(
            pl.BlockSpec((tm, d), lambda i, c: (i, 0)),
            pl.BlockSpec((None, ut_rows, FFN_TC), lambda i, c: (i, 0, c)),
        ),
        scratch_shapes=[
            pltpu.VMEM((tm, d), BF16),
            pltpu.VMEM((tm, d), F32),
            pltpu.VMEM((2, d_ff), F32),
        ],
        compiler_params=_cparams(("arbitrary", "arbitrary")),
        name="ffn",
    )(x2, g, sc, sh, gt, wup, wup, cw, cb, wd, e1, e2)


def _group_mods(mod, d, rows_per_seq, tile_rows):
    parts = [mod[:, k * d:(k + 1) * d] for k in range(6)]
    if rows_per_seq >= tile_rows:
        return [p[:, None, :] for p in parts]
    return [jnp.repeat(p, rows_per_seq, axis=0)[None] for p in parts]


def _layer(x3, mod, p, layer, *, attend, s0, conv_prev, tm, consts):
    nseq, seq, d = x3.shape
    rows = nseq * seq
    x2 = x3.reshape(rows, d)
    tiles_per_group = max(seq // tm, 1) if seq >= tm else rows // tm
    sh1, sc1, gt1, sh2, sc2, gt2 = _group_mods(mod, d, seq, tm)

    qkv, kv, hg, r = _in_proj(x2, p['g_norm1'], sc1, sh1, p['w_in'], consts['bd'], p['q_gain'], p['k_gain'],
                              tm=tm, tiles_per_group=tiles_per_group)
    oa = attend(qkv, kv)

    if seq >= HGRN_CHUNK:
        ob3, s_new = _hgrn(hg.reshape(nseq, seq, 4 * W_B), s0, p['lb_logits'], p['hgrn_gain'],
                           layer=layer, chunk=HGRN_CHUNK, valid=HGRN_CHUNK, nseq=1, nchunks=4)
        ob = ob3.reshape(rows, W_B)
    else:
        padded = 8
        hg3 = jnp.pad(hg.reshape(nseq, seq, 4 * W_B), ((0, 0), (0, padded - seq), (0, 0)))
        ob3, s_new = _hgrn(hg3, s0, p['lb_logits'], p['hgrn_gain'],
                           layer=layer, chunk=padded, valid=seq, nseq=4, nchunks=1)
        ob = ob3[:, :seq].reshape(rows, W_B)

    x1 = _mix(oa, ob, r, x2, gt1, p['w_pa'], p['w_pb'], p['w_o'], tm=tm, tiles_per_group=tiles_per_group)

    d_ff = p['w_down'].shape[0]
    if seq >= tm:
        e1 = e2 = conv_prev
        ut_rows = 8
    else:
        z = jnp.zeros((nseq, seq - 2, d_ff), F32)
        e1 = jnp.concatenate([conv_prev[:, 1:2], jnp.zeros((nseq, seq - 1, d_ff), F32)], axis=1).reshape(rows, d_ff)
        e2 = jnp.concatenate([conv_prev, z], axis=1).reshape(rows, d_ff)
        ut_rows = tm
    y, ut = _ffn(x1, p['g_norm2'], sc2, sh2, gt2, p['w_up'], p['conv_w'], p['conv_b'], p['w_down'], e1, e2,
                 tm=tm, tiles_per_group=tiles_per_group, seq=seq, ut_rows=ut_rows)
    if seq >= tm:
        tps = seq // tm
        conv_new = ut.reshape(nseq, tps, 8, d_ff)[:, tps - 1, 6:8]
    else:
        conv_new = ut.reshape(nseq, seq, d_ff)[:, seq - 2:]
    k_out = kv[0].reshape(nseq, seq, H_A, DH_A)
    v_out = kv[1].reshape(nseq, seq, H_A, DH_A)
    return y.reshape(nseq, seq, d), k_out, v_out, s_new, conv_new


def kernel(x_prompt, x_sample, cache_k, cache_v, state_hgrn, state_conv, page_table, c_prompt, c_sample,
           w_ada, b_ada, g_norm1, w_in, q_gain, k_gain, sb_bias, hgrn_lb_logits, hgrn_gain, w_pa, w_pb, w_o,
           g_norm2, w_up, conv_w, conv_b, w_down):
    depth = w_ada.shape[0]
    n_pr, seq, d = x_prompt.shape
    n_dec, dec_seq, _ = x_sample.shape
    d_ff = w_down.shape[1]
    n_pool, page = cache_k.shape[1], cache_k.shape[2]

    mod_all = _ada(jnp.concatenate([c_prompt, c_sample], axis=0), w_ada, b_ada)

    head = np.arange(W_A) // DH_A
    consts = {'bd': jnp.asarray((head[:, None] == head[None, :]).astype(np.float32) / DH_A, dtype=BF16)}
    uo_prompt = _suffix_matrix(ATT_TK)
    uo_page = _suffix_matrix(page)
    ck = cache_k.reshape(depth, n_pool, page, W_A)
    cv = cache_v.reshape(depth, n_pool, page, W_A)
    rowh = np.arange(dec_seq * H_A) % H_A
    qmask = jnp.asarray((rowh[:, None] == head[None, :]), dtype=BF16)

    yp, ys = x_prompt, x_sample
    outs = [[] for _ in range(8)]
    for l in range(depth):
        p = {
            'g_norm1': g_norm1[l][None], 'g_norm2': g_norm2[l][None],
            'w_in': w_in[l].astype(BF16), 'w_pa': w_pa[l].astype(BF16), 'w_pb': w_pb[l].astype(BF16),
            'w_o': w_o[l].astype(BF16), 'w_up': w_up[l].astype(BF16), 'w_down': w_down[l].astype(BF16),
            'q_gain': jnp.tile(q_gain[l], H_A)[None], 'k_gain': jnp.tile(k_gain[l], H_A)[None],
            'lb_logits': hgrn_lb_logits, 'hgrn_gain': hgrn_gain[l][None],
            'conv_w': conv_w[l], 'conv_b': conv_b[l][None],
        }

        def attend_prompt(qkv, kv, l=l):
            return _attn_prompt(qkv, sb_bias[l], uo_prompt, batch=n_pr, seq=seq)

        def attend_sample(qkv, kv, l=l):
            q = qkv[:, :W_A].reshape(n_dec, dec_seq, 1, W_A)
            qbd = (jnp.broadcast_to(q, (n_dec, dec_seq, H_A, W_A)).reshape(n_dec, dec_seq * H_A, W_A)
                   * qmask[None])
            bias_rows = jnp.broadcast_to(jnp.tile(sb_bias[l], dec_seq)[:, None], (dec_seq * H_A, V7X_LANES))
            knew = jnp.pad(kv[0].reshape(n_dec, dec_seq, W_A), ((0, 0), (0, 8 - dec_seq), (0, 0)))
            vnew = jnp.pad(kv[1].reshape(n_dec, dec_seq, W_A), ((0, 0), (0, 8 - dec_seq), (0, 0)))
            o = _attn_sample(page_table, qbd, bias_rows, knew, vnew, uo_page, ck, cv, layer=l)
            return o.reshape(n_dec * dec_seq, W_A).astype(BF16)

        yp, kp, vp, hp, cp = _layer(
            yp, mod_all[l, :n_pr], p, l, attend=attend_prompt,
            s0=jnp.zeros((n_pr, H_B, DK_B, DK_B), F32), conv_prev=jnp.zeros((n_pr, CONV_W - 1, d_ff), F32),
            tm=512, consts=consts)
        ys, ksm, vsm, hs, cs = _layer(
            ys, mod_all[l, n_pr:], p, l, attend=attend_sample,
            s0=state_hgrn[l], conv_prev=state_conv[l], tm=n_dec * dec_seq, consts=consts)
        for lst, val in zip(outs, (kp, vp, ksm, vsm, hp, hs, cp, cs)):
            lst.append(val)

    return (yp, ys) + tuple(jnp.stack(o) for o in outs)
```

```python
import functools

import numpy as np
import jax
import jax.numpy as jnp
from jax import lax
from jax.experimental import pallas as pl
from jax.experimental.pallas import tpu as pltpu

F32 = jnp.float32
BF16 = jnp.bfloat16

EPS = 1e-6
V7X_LANES = 128
V7X_VMEM_LIMIT_BYTES = 56 * 1024 * 1024

H_A = 8
DH_A = 64
W_A = H_A * DH_A
H_B = 4
DK_B = 128
W_B = H_B * DK_B
CONV_W = 3
HGRN_CHUNK = 64


def _cparams(sem):
    return pltpu.CompilerParams(dimension_semantics=sem, vmem_limit_bytes=V7X_VMEM_LIMIT_BYTES)


def _sigmoid(x):
    return 1.0 / (1.0 + jnp.exp(-x))


def _dot(a, b):
    return jnp.dot(a, b, preferred_element_type=F32)


def _dot_nt(a, b):
    return lax.dot_general(a, b, (((1,), (1,)), ((), ())), preferred_element_type=F32)


def _split2(x):
    hi = x.astype(BF16)
    lo = (x - hi.astype(F32)).astype(BF16)
    return hi, lo


def _ada_kernel(c_ref, w_ref, b_ref, o_ref):
    c = c_ref[...]
    s = c * _sigmoid(c)
    s_hi, s_lo = _split2(s)
    w_hi, w_lo = _split2(w_ref[...])
    acc = _dot(s_hi, w_hi) + _dot(s_hi, w_lo) + _dot(s_lo, w_hi)
    o_ref[...] = acc + b_ref[...]


def _ada(c_all, w_ada, b_ada):
    depth, d, n6 = w_ada.shape
    nc = c_all.shape[0]
    tn = 1536
    return pl.pallas_call(
        _ada_kernel,
        out_shape=jax.ShapeDtypeStruct((depth, nc, n6), F32),
        grid=(depth, n6 // tn),
        in_specs=[
            pl.BlockSpec((nc, d), lambda l, j: (0, 0)),
            pl.BlockSpec((None, d, tn), lambda l, j: (l, 0, j)),
            pl.BlockSpec((None, 1, tn), lambda l, j: (l, 0, j)),
        ],
        out_specs=pl.BlockSpec((None, nc, tn), lambda l, j: (l, 0, j)),
        compiler_params=_cparams(("parallel", "parallel")),
        name="ada_mod",
    )(c_all, w_ada, b_ada.reshape(depth, 1, n6))


IN_TN = 512


def _in_kernel(x_ref, g_ref, sc_ref, sh_ref, w_ref, wkv_ref, bd_ref, qg_ref, kg_ref,
               q_ref, kv_ref, hg_ref, r_ref, h_scr, *, kv_t):
    j = pl.program_id(1)

    @pl.when(j == 0)
    def _():
        x = x_ref[...]
        ms = jnp.mean(x * x, axis=-1, keepdims=True)
        y = x * lax.rsqrt(ms + EPS) * g_ref[...]
        h_scr[...] = (y * (1.0 + sc_ref[...]) + sh_ref[...]).astype(BF16)

    def proj():
        return _dot(h_scr[...], w_ref[...])

    def proj_t():
        return _dot_nt(wkv_ref[...], h_scr[...])

    def head_norm(t):
        ms = _dot((t * t).astype(BF16), bd_ref[...])
        return t * lax.rsqrt(ms + EPS)

    @pl.when(j == 0)
    def _():
        q_ref[...] = (head_norm(proj()) * qg_ref[...] * (DH_A ** -0.5)).astype(BF16)

    @pl.when(j == 1)
    def _():
        if kv_t:
            t = proj_t()
            t3 = t.reshape(H_A, DH_A, t.shape[1])
            ms = jnp.mean(t3 * t3, axis=1, keepdims=True)
            kv_ref[...] = (t3 * lax.rsqrt(ms + EPS)).reshape(t.shape) * kg_ref[...]
        else:
            t = proj()
            sq_hi, sq_lo = _split2(t * t)
            ms = _dot(sq_hi, bd_ref[...]) + _dot(sq_lo, bd_ref[...])
            kv_ref[...] = t * lax.rsqrt(ms + EPS) * kg_ref[...]

    @pl.when(j == 2)
    def _():
        kv_ref[...] = proj_t() if kv_t else proj()

    @pl.when(jnp.logical_and(j >= 3, j <= 6))
    def _():
        hg_ref[...] = proj()

    @pl.when(j >= 7)
    def _():
        r_ref[...] = _sigmoid(proj())


def _in_proj(x2, g, sc, sh, w_bf, wkv_t, bd, qg, kg, *, tm, tiles_per_group, kv_t):
    rows, d = x2.shape
    n_in = w_bf.shape[1]
    nj = n_in // IN_TN
    rb = sc.shape[1]
    mod_spec = pl.BlockSpec((None, rb, d), lambda i, j: (i // tiles_per_group, 0, 0))
    kv_sel = lambda j: jnp.clip(j - 1, 0, 1)
    if kv_t:
        nseq = rows // (tiles_per_group * tm)
        kv_shape = (2, nseq, W_A, tiles_per_group * tm)
        kv_spec = pl.BlockSpec((None, None, W_A, tm),
                               lambda i, j: (kv_sel(j), i // tiles_per_group, 0, i % tiles_per_group))
        w_spec = pl.BlockSpec((d, IN_TN), lambda i, j: (0, jnp.where((j == 1) | (j == 2), 0, j)))
        kg_spec = pl.BlockSpec((W_A, 1), lambda i, j: (0, 0))
    else:
        kv_shape = (2, rows, W_A)
        kv_spec = pl.BlockSpec((None, tm, IN_TN), lambda i, j: (kv_sel(j), i, 0))
        w_spec = pl.BlockSpec((d, IN_TN), lambda i, j: (0, j))
        kg_spec = pl.BlockSpec((1, W_A), lambda i, j: (0, 0))
    return pl.pallas_call(
        functools.partial(_in_kernel, kv_t=kv_t),
        out_shape=(
            jax.ShapeDtypeStruct((rows, W_A), BF16),
            jax.ShapeDtypeStruct(kv_shape, F32),
            jax.ShapeDtypeStruct((rows, 4 * W_B), F32),
            jax.ShapeDtypeStruct((rows, 2 * d), F32),
        ),
        grid=(rows // tm, nj),
        in_specs=[
            pl.BlockSpec((tm, d), lambda i, j: (i, 0)),
            pl.BlockSpec((1, d), lambda i, j: (0, 0)),
            mod_spec,
            mod_spec,
            w_spec,
            pl.BlockSpec((None, W_A, d), lambda i, j: (kv_sel(j), 0, 0)),
            pl.BlockSpec((W_A, W_A), lambda i, j: (0, 0)),
            pl.BlockSpec((1, W_A), lambda i, j: (0, 0)),
            kg_spec,
        ],
        out_specs=(
            pl.BlockSpec((tm, IN_TN), lambda i, j: (i, 0)),
            kv_spec,
            pl.BlockSpec((tm, IN_TN), lambda i, j: (i, jnp.clip(j - 3, 0, 3))),
            pl.BlockSpec((tm, IN_TN), lambda i, j: (i, jnp.clip(j - 7, 0, 3))),
        ),
        scratch_shapes=[pltpu.VMEM((tm, d), BF16)],
        compiler_params=_cparams(("parallel", "arbitrary")),
        name="in_proj",
    )(x2, g, sc, sh, w_bf, wkv_t, bd, qg, kg)


def _sb_keep(z, mask):
    e = jnp.exp(-jnp.abs(z))
    log_keep = -(jnp.maximum(z, 0.0) + jnp.log(1.0 + e))
    lk = log_keep if mask is None else jnp.where(mask, log_keep, 0.0)
    return lk, z + log_keep


def _sb_weights(log_beta, after, c, mask):
    n = log_beta.shape[1]
    c_full = c if n == V7X_LANES else jnp.concatenate([c] * (n // V7X_LANES), axis=1)
    a = jnp.exp(log_beta + after + c_full)
    return a if mask is None else jnp.where(mask, a, 0.0)


def _sb_blocks(zs, c, u, masks):
    keeps = [_sb_keep(z, m) for z, m in zip(zs, masks)]
    afters = [_dot(lk.astype(BF16), u) for lk, _ in keeps]
    tots = [jnp.sum(lk, axis=1, keepdims=True) for lk, _ in keeps]
    ws = []
    for (_, log_beta), after, tot, m in zip(keeps, afters, tots, masks):
        ws.append(_sb_weights(log_beta, after, c, m).astype(BF16))
        c = c + tot
    return ws, c


def _suffix_matrix(n):
    j = np.arange(n)[:, None]
    s = np.arange(n)[None, :]
    return jnp.asarray((j > s).astype(np.float32), dtype=BF16)


ATT_TQ = 256
ATT_TK = 256


def _attn_prompt_kernel(bias_ref, q_ref, kt_ref, vt_ref, u_ref, o_ref, kv_scr, acc_scr, c_scr):
    hp = pl.program_id(1)
    qi = pl.program_id(2)
    tq, tk = ATT_TQ, ATT_TK
    nkb = kv_scr.shape[1]
    first_head = lax.broadcasted_iota(jnp.int32, (1, V7X_LANES), 1) < DH_A

    @pl.when(qi == 0)
    def _():
        for kb in range(nkb):
            kv_scr[0, kb] = kt_ref[:, kb * tk:(kb + 1) * tk].astype(BF16)
            kv_scr[1, kb] = vt_ref[:, kb * tk:(kb + 1) * tk].astype(BF16)

    q = q_ref[...]
    zero = jnp.zeros_like(q)
    q2 = jnp.concatenate([jnp.where(first_head, q, zero), jnp.where(first_head, zero, q)], axis=0)
    u = u_ref[...]
    b0 = bias_ref[2 * hp]
    b1 = bias_ref[2 * hp + 1]
    row = lax.broadcasted_iota(jnp.int32, (2 * tq, tk), 0) & (tq - 1)
    col = lax.broadcasted_iota(jnp.int32, (2 * tq, tk), 1)

    def steps(kbs, masks):
        zs = []
        for kb in kbs:
            s = _dot(q2, kv_scr[0, kb])
            zs.append(jnp.concatenate([s[:tq] + b0, s[tq:] + b1], axis=0))
        ws, c = _sb_blocks(zs, c_scr[...], u, masks)
        acc = acc_scr[...]
        for kb, w in zip(kbs, ws):
            acc = acc + _dot_nt(w, kv_scr[1, kb])
        acc_scr[...] = acc
        c_scr[...] = c

    acc_scr[...] = jnp.zeros_like(acc_scr)
    c_scr[...] = jnp.zeros_like(c_scr)
    steps([qi], [col < row])

    def body(i, carry):
        kb = qi - 1 - 2 * i
        steps([kb, kb - 1], [None, None])
        return carry

    lax.fori_loop(0, qi // 2, body, 0)

    @pl.when((qi & 1) == 1)
    def _():
        steps([0], [None])

    o_ref[...] = jnp.where(first_head, acc_scr[:tq], acc_scr[tq:]).astype(o_ref.dtype)


def _attn_prompt(q, kvt, sb_bias, u, *, batch, seq):
    rows = q.shape[0]
    nq = seq // ATT_TQ
    npair = H_A // 2
    return pl.pallas_call(
        _attn_prompt_kernel,
        out_shape=jax.ShapeDtypeStruct((rows, W_A), BF16),
        grid=(batch, npair, nq),
        in_specs=[
            pl.BlockSpec(memory_space=pltpu.SMEM),
            pl.BlockSpec((ATT_TQ, V7X_LANES), lambda b, hp, qi: (b * nq + qi, hp)),
            pl.BlockSpec((None, None, V7X_LANES, seq), lambda b, hp, qi: (0, b, hp, 0)),
            pl.BlockSpec((None, None, V7X_LANES, seq), lambda b, hp, qi: (1, b, hp, 0)),
            pl.BlockSpec((ATT_TK, ATT_TK), lambda b, hp, qi: (0, 0)),
        ],
        out_specs=pl.BlockSpec((ATT_TQ, V7X_LANES), lambda b, hp, qi: (b * nq + qi, hp)),
        scratch_shapes=[
            pltpu.VMEM((2, seq // ATT_TK, V7X_LANES, ATT_TK), BF16),
            pltpu.VMEM((2 * ATT_TQ, V7X_LANES), F32),
            pltpu.VMEM((2 * ATT_TQ, V7X_LANES), F32),
        ],
        compiler_params=_cparams(("parallel", "parallel", "arbitrary")),
        name="attn_prompt",
    )(sb_bias, q, kvt, kvt, u)


PAGES_PER_STEP = 8


def _attn_sample_kernel(pt_ref, qbd_ref, bias_ref, knew_ref, vnew_ref, u_ref, *rest):
    npg = PAGES_PER_STEP
    k_refs = rest[:npg]
    v_refs = rest[npg:2 * npg]
    o_ref = rest[2 * npg]
    acc_scr, c_scr = rest[2 * npg + 1:]
    p = pl.program_id(1)
    nrow = qbd_ref.shape[0]
    dec_seq = nrow // H_A
    page = u_ref.shape[0]
    qbd = qbd_ref[...]
    bias = bias_ref[...]
    u = u_ref[...]

    @pl.when(p == 0)
    def _():
        pad = jnp.zeros((page - knew_ref.shape[0], W_A), F32)
        kn = jnp.concatenate([knew_ref[...], pad], axis=0).astype(BF16)
        vn = jnp.concatenate([vnew_ref[...], pad], axis=0).astype(BF16)
        t = lax.broadcasted_iota(jnp.int32, (nrow, page), 0) >> 3
        s = lax.broadcasted_iota(jnp.int32, (nrow, page), 1)
        ws, c = _sb_blocks([_dot_nt(qbd, kn) + bias], jnp.zeros((nrow, V7X_LANES), F32), u, [s < t])
        acc_scr[...] = _dot(ws[0], vn)
        c_scr[...] = c

    zs = [_dot(qbd, k_refs[i][...].astype(BF16)) + bias for i in range(npg)]
    ws, c = _sb_blocks(zs, c_scr[...], u, [None] * npg)
    acc = acc_scr[...]
    for i in range(npg):
        acc = acc + _dot_nt(ws[i], v_refs[i][...].astype(BF16))
    acc_scr[...] = acc
    c_scr[...] = c

    @pl.when(p == pl.num_programs(1) - 1)
    def _():
        r = lax.broadcasted_iota(jnp.int32, (nrow, W_A), 0)
        l = lax.broadcasted_iota(jnp.int32, (nrow, W_A), 1)
        own = (r & (H_A - 1)) == (l >> 6)
        om = jnp.where(own, acc_scr[...], 0.0)
        o_ref[...] = jnp.sum(om.reshape(dec_seq, H_A, W_A), axis=1)


def _attn_sample(page_table, qbd, bias_rows, knew, vnew, uo, cache_k, cache_v, *, layer):
    nb, nrow, _ = qbd.shape
    n_pages = page_table.shape[1]
    page = cache_k.shape[3]
    nsteps = n_pages // PAGES_PER_STEP
    dec_seq = nrow // H_A

    def page_spec(i):
        def imap(b, p, pt):
            return (layer, pt[b, n_pages - 1 - (p * PAGES_PER_STEP + i)], 0, 0)
        return pl.BlockSpec((None, None, W_A, page), imap)

    grid_spec = pltpu.PrefetchScalarGridSpec(
        num_scalar_prefetch=1,
        grid=(nb, nsteps),
        in_specs=[
            pl.BlockSpec((None, nrow, W_A), lambda b, p, pt: (b, 0, 0)),
            pl.BlockSpec((nrow, V7X_LANES), lambda b, p, pt: (0, 0)),
            pl.BlockSpec((None, 8, W_A), lambda b, p, pt: (b, 0, 0)),
            pl.BlockSpec((None, 8, W_A), lambda b, p, pt: (b, 0, 0)),
            pl.BlockSpec((page, page), lambda b, p, pt: (0, 0)),
        ] + [page_spec(i) for i in range(PAGES_PER_STEP)] + [page_spec(i) for i in range(PAGES_PER_STEP)],
        out_specs=pl.BlockSpec((None, dec_seq, W_A), lambda b, p, pt: (b, 0, 0)),
        scratch_shapes=[
            pltpu.VMEM((nrow, W_A), F32),
            pltpu.VMEM((nrow, V7X_LANES), F32),
        ],
    )
    return pl.pallas_call(
        _attn_sample_kernel,
        out_shape=jax.ShapeDtypeStruct((nb, dec_seq, W_A), F32),
        grid_spec=grid_spec,
        compiler_params=_cparams(("parallel", "arbitrary")),
        name="attn_sample",
    )(page_table, qbd, bias_rows, knew, vnew, uo,
      *([cache_k] * PAGES_PER_STEP), *([cache_v] * PAGES_PER_STEP))


def _seg_bcast(x, n, off):
    rows, lanes = x.shape
    if n >= 8:
        parts = [jnp.broadcast_to(x[b * n + off:b * n + off + 1, :], (n, lanes)) for b in range(rows // n)]
        return jnp.concatenate(parts, axis=0)
    sub = lax.broadcasted_iota(jnp.int32, (8, lanes), 0)
    parts = []
    for g in range(rows // 8):
        acc = jnp.broadcast_to(x[g * 8 + off:g * 8 + off + 1, :], (8, lanes))
        for sb in range(1, 8 // n):
            r = g * 8 + sb * n + off
            acc = jnp.where(sub >= sb * n, jnp.broadcast_to(x[r:r + 1, :], (8, lanes)), acc)
        parts.append(acc)
    return jnp.concatenate(parts, axis=0)


def _hgrn_levels(chunk):
    out = []
    n = chunk
    while n >= 2:
        out.append(n)
        n //= 2
    return out


def _hgrn_masks(chunk, groups):
    rows = chunk * groups
    t = np.arange(rows)
    m = [(t[:, None] // n == t[None, :] // n) for n in _hgrn_levels(chunk)]
    m.append(t[:, None] == t[None, :])
    return jnp.asarray(np.stack(m).astype(np.float32))


def _hgrn_kernel(lbl_ref, gain_ref, masks_ref, tri_ref, hg_ref, s0_ref, ob_ref, sout_ref, s_scr,
                 *, layer, chunk, valid, nseq, nchunks):
    j = pl.program_id(1)
    groups = nseq * H_B
    rows = groups * chunk
    levels = _hgrn_levels(chunk)

    @pl.when(j == 0)
    def _():
        s_scr[...] = s0_ref[...].reshape(groups, DK_B, DK_B)

    lg = lbl_ref[...]
    ex = jnp.exp(lg - jnp.max(lg, axis=0, keepdims=True))
    wts = ex / jnp.sum(ex, axis=0, keepdims=True)
    lb_row = jnp.sum(wts[:layer + 1], axis=0, keepdims=True) - wts[0:1]

    def stack(get):
        return jnp.concatenate([get(s, h) for s in range(nseq) for h in range(H_B)], axis=0)

    lbs = stack(lambda s, h: jnp.broadcast_to(lb_row[:, h * DK_B:(h + 1) * DK_B], (chunk, DK_B)))
    tloc = lax.broadcasted_iota(jnp.int32, (rows, DK_B), 0) & (chunk - 1)
    tri = tri_ref[...]
    gain = gain_ref[...]

    for ci in range(nchunks):
        r0 = ci * chunk

        def seg(col, r0=r0):
            return stack(lambda s, h: hg_ref[s, r0:r0 + chunk, (col * H_B + h) * DK_B:(col * H_B + h + 1) * DK_B])

        zq, zf, vi, zg = seg(0), seg(1), seg(2), seg(3)
        logf = jnp.log(lbs + (1.0 - lbs) * _sigmoid(zf))
        kk = (1.0 - lbs) * _sigmoid(-zf)
        qs = zq * _sigmoid(zq)
        if valid < chunk:
            ok = tloc < valid
            logf = jnp.where(ok, logf, 0.0)
            kk = jnp.where(ok, kk, 0.0)
            qs = jnp.where(ok, qs, 0.0)

        l_hi = logf.astype(BF16)
        rem = logf - l_hi.astype(F32)
        l_mid = rem.astype(BF16)
        l_lo = (rem - l_mid.astype(F32)).astype(BF16)
        cum = _dot(tri, l_hi) + _dot(tri, l_mid) + _dot(tri, l_lo)

        att = _dot_nt(qs.astype(BF16), kk.astype(BF16)) * masks_ref[len(levels)]
        for li, n in enumerate(levels):
            ref_row = _seg_bcast(cum, n, n // 2 - 1)
            e = jnp.exp(-jnp.abs(cum - ref_row))
            upper = (tloc & (n - 1)) >= (n // 2)
            qn = jnp.where(upper, qs * e, 0.0).astype(BF16)
            kn = jnp.where(upper, 0.0, kk * e).astype(BF16)
            att = att + _dot_nt(qn, kn) * masks_ref[li]

        o = _dot(att.astype(BF16), vi.astype(BF16))
        qe = qs * jnp.exp(cum)
        o = o + jnp.concatenate(
            [_dot(qe[g * chunk:(g + 1) * chunk].astype(BF16), s_scr[g].astype(BF16)) for g in range(groups)],
            axis=0)

        on = o * lax.rsqrt(jnp.mean(o * o, axis=-1, keepdims=True) + EPS) * gain
        og = on * (zg * _sigmoid(zg))
        for s in range(nseq):
            ob_ref[s, r0:r0 + chunk, :] = jnp.concatenate(
                [og[(s * H_B + h) * chunk:(s * H_B + h + 1) * chunk] for h in range(H_B)], axis=1)

        last = _seg_bcast(cum, chunk, chunk - 1)
        kd_t = (kk * jnp.exp(last - cum)).T
        dec_t = jnp.exp(last).T
        colg = lax.broadcasted_iota(jnp.int32, (DK_B, rows), 1) >> (chunk.bit_length() - 1)
        vb = vi.astype(BF16)
        for g in range(groups):
            kd_g = jnp.where(colg == g, kd_t, 0.0).astype(BF16)
            s_scr[g] = dec_t[:, g * chunk:g * chunk + 1] * s_scr[g] + _dot(kd_g, vb)

    @pl.when(j == pl.num_programs(1) - 1)
    def _():
        sout_ref[...] = s_scr[...].reshape(nseq, H_B, DK_B, DK_B)


def _hgrn(hg3, s0, lb_logits, gain, *, layer, chunk, valid, nseq, nchunks):
    nb, tp, _ = hg3.shape
    rb = chunk * nchunks
    groups = nseq * H_B
    rows = groups * chunk
    masks = _hgrn_masks(chunk, groups)
    t = np.arange(rows)
    tri = jnp.asarray(((t[:, None] // chunk == t[None, :] // chunk) & (t[None, :] <= t[:, None])).astype(np.float32),
                      dtype=BF16)
    kern = functools.partial(_hgrn_kernel, layer=layer, chunk=chunk, valid=valid, nseq=nseq, nchunks=nchunks)
    return pl.pallas_call(
        kern,
        out_shape=(
            jax.ShapeDtypeStruct((nb, tp, W_B), F32),
            jax.ShapeDtypeStruct((nb, H_B, DK_B, DK_B), F32),
        ),
        grid=(nb // nseq, tp // rb),
        in_specs=[
            pl.BlockSpec(lb_logits.shape, lambda b, j: (0, 0)),
            pl.BlockSpec((1, DK_B), lambda b, j: (0, 0)),
            pl.BlockSpec(masks.shape, lambda b, j: (0, 0, 0)),
            pl.BlockSpec((rows, rows), lambda b, j: (0, 0)),
            pl.BlockSpec((nseq, rb, 4 * W_B), lambda b, j: (b, j, 0)),
            pl.BlockSpec((nseq, H_B, DK_B, DK_B), lambda b, j: (b, 0, 0, 0)),
        ],
        out_specs=(
            pl.BlockSpec((nseq, rb, W_B), lambda b, j: (b, j, 0)),
            pl.BlockSpec((nseq, H_B, DK_B, DK_B), lambda b, j: (b, 0, 0, 0)),
        ),
        scratch_shapes=[pltpu.VMEM((groups, DK_B, DK_B), F32)],
        compiler_params=_cparams(("parallel", "arbitrary")),
        name="hgrn",
    )(lb_logits, gain, masks, tri, hg3, s0)


def _mix_kernel(oa_ref, ob_ref, r_ref, x_ref, gt_ref, wpa_ref, wpb_ref, wo_ref, out_ref):
    d = x_ref.shape[1]
    pa = _dot(oa_ref[...], wpa_ref[...])
    pb = _dot(ob_ref[...].astype(BF16), wpb_ref[...])
    merged = r_ref[:, :d] * pa + r_ref[:, d:] * pb
    y = _dot(merged.astype(BF16), wo_ref[...])
    out_ref[...] = x_ref[...] + gt_ref[...] * y


def _mix(oa, ob, r, x2, gt, wpa, wpb, wo, *, tm, tiles_per_group):
    rows, d = x2.shape
    rb = gt.shape[1]
    return pl.pallas_call(
        _mix_kernel,
        out_shape=jax.ShapeDtypeStruct((rows, d), F32),
        grid=(rows // tm,),
        in_specs=[
            pl.BlockSpec((tm, W_A), lambda i: (i, 0)),
            pl.BlockSpec((tm, W_B), lambda i: (i, 0)),
            pl.BlockSpec((tm, 2 * d), lambda i: (i, 0)),
            pl.BlockSpec((tm, d), lambda i: (i, 0)),
            pl.BlockSpec((None, rb, d), lambda i: (i // tiles_per_group, 0, 0)),
            pl.BlockSpec((W_A, d), lambda i: (0, 0)),
            pl.BlockSpec((W_B, d), lambda i: (0, 0)),
            pl.BlockSpec((d, d), lambda i: (0, 0)),
        ],
        out_specs=pl.BlockSpec((tm, d), lambda i: (i, 0)),
        compiler_params=_cparams(("parallel",)),
        name="mix",
    )(oa, ob, r, x2, gt, wpa, wpb, wo)


FFN_CHUNKS = 2


def _gelu_tanh(x):
    return 0.5 * x * (1.0 + jnp.tanh(0.7978845608028654 * (x + 0.044715 * x * x * x)))


def _ffn_kernel(x_ref, g_ref, sc_ref, sh_ref, gt_ref, wu_ref, wv_ref, cw_ref, cb_ref, wd_ref, e1_ref, e2_ref,
                out_ref, ut_ref, h_scr, acc_scr, carry_scr, *, seq, tiles_per_seq):
    i = pl.program_id(0)
    c = pl.program_id(1)
    tm = x_ref.shape[0]
    tc = wu_ref.shape[1]

    @pl.when(c == 0)
    def _():
        x = x_ref[...]
        ms = jnp.mean(x * x, axis=-1, keepdims=True)
        y = x * lax.rsqrt(ms + EPS) * g_ref[...]
        h_scr[...] = (y * (1.0 + sc_ref[...]) + sh_ref[...]).astype(BF16)
        acc_scr[...] = jnp.zeros_like(acc_scr)

    h = h_scr[...]
    u = _dot(h, wu_ref[...])
    v = _dot(h, wv_ref[...])
    row = lax.broadcasted_iota(jnp.int32, (tm, tc), 0)
    col0 = pl.multiple_of(c * tc, tc)
    if tiles_per_seq >= 1:
        @pl.when((i % tiles_per_seq) == 0)
        def _():
            carry_scr[:, pl.ds(col0, tc)] = e1_ref[...]

        prev = carry_scr[:, pl.ds(col0, tc)]
        m1 = jnp.where(row >= 1, pltpu.roll(u, 1, 0), jnp.broadcast_to(prev[1:2], (tm, tc)))
        m2 = jnp.where(row >= 2, pltpu.roll(u, 2, 0),
                       jnp.where(row == 1, jnp.broadcast_to(prev[1:2], (tm, tc)),
                                 jnp.broadcast_to(prev[0:1], (tm, tc))))
        carry_scr[:, pl.ds(col0, tc)] = u[tm - 2:tm, :]
    else:
        t = row & (seq - 1)
        m1 = jnp.where(t >= 1, pltpu.roll(u, 1, 0), e1_ref[...])
        m2 = jnp.where(t >= 2, pltpu.roll(u, 2, 0), e2_ref[...])
    conv = cb_ref[...] + m2 * cw_ref[0:1, :] + m1 * cw_ref[1:2, :] + u * cw_ref[2:3, :]
    act = (_gelu_tanh(conv) * v).astype(BF16)
    acc_scr[...] = acc_scr[...] + _dot(act, wd_ref[...])
    ut_ref[...] = u[tm - ut_ref.shape[0]:, :]

    @pl.when(c == pl.num_programs(1) - 1)
    def _():
        out_ref[...] = x_ref[...] + gt_ref[...] * acc_scr[...]


def _ffn(x2, g, sc, sh, gt, wup, cw, cb, wd, e1, e2, *, tm, tiles_per_group, seq, ut_rows):
    rows, d = x2.shape
    d_ff = wd.shape[0]
    nc = FFN_CHUNKS
    tc = d_ff // nc
    assert tc * nc == d_ff and tc % V7X_LANES == 0
    rb = sc.shape[1]
    tiles_per_seq = seq // tm
    mod_spec = pl.BlockSpec((None, rb, d), lambda i, c: (i // tiles_per_group, 0, 0))
    if tiles_per_seq >= 1:
        e_spec = pl.BlockSpec((None, 2, tc), lambda i, c: (i // tiles_per_seq, 0, c))
    else:
        e_spec = pl.BlockSpec((tm, tc), lambda i, c: (i, c))
    kern = functools.partial(_ffn_kernel, seq=seq, tiles_per_seq=tiles_per_seq)
    return pl.pallas_call(
        kern,
        out_shape=(
            jax.ShapeDtypeStruct((rows, d), F32),
            jax.ShapeDtypeStruct((rows // tm, ut_rows, d_ff), F32),
        ),
        grid=(rows // tm, nc),
        in_specs=[
            pl.BlockSpec((tm, d), lambda i, c: (i, 0)),
            pl.BlockSpec((1, d), lambda i, c: (0, 0)),
            mod_spec, mod_spec, mod_spec,
            pl.BlockSpec((d, tc), lambda i, c: (0, c)),
            pl.BlockSpec((d, tc), lambda i, c: (0, nc + c)),
            pl.BlockSpec((CONV_W, tc), lambda i, c: (0, c)),
            pl.BlockSpec((1, tc), lambda i, c: (0, c)),
            pl.BlockSpec((tc, d), lambda i, c: (c, 0)),
            e_spec, e_spec,
        ],
        out_specs=(
            pl.BlockSpec((tm, d), lambda i, c: (i, 0)),
            pl.BlockSpec((None, ut_rows, tc), lambda i, c: (i, 0, c)),
        ),
        scratch_shapes=[
            pltpu.VMEM((tm, d), BF16),
            pltpu.VMEM((tm, d), F32),
            pltpu.VMEM((2, d_ff), F32),
        ],
        compiler_params=_cparams(("arbitrary", "arbitrary")),
        name="ffn",
    )(x2, g, sc, sh, gt, wup, wup, cw, cb, wd, e1, e2)


def _group_mods(mod, d, rows_per_seq, tile_rows):
    parts = [mod[:, k * d:(k + 1) * d] for k in range(6)]
    if rows_per_seq >= tile_rows:
        return [p[:, None, :] for p in parts]
    return [jnp.repeat(p, rows_per_seq, axis=0)[None] for p in parts]


def _layer(x3, mod, p, layer, *, attend, s0, conv_prev, tm, consts):
    nseq, seq, d = x3.shape
    rows = nseq * seq
    x2 = x3.reshape(rows, d)
    tiles_per_group = max(seq // tm, 1) if seq >= tm else rows // tm
    sh1, sc1, gt1, sh2, sc2, gt2 = _group_mods(mod, d, seq, tm)

    kv_t = seq >= tm
    k_gain = p['k_gain'].reshape(W_A, 1) if kv_t else p['k_gain']
    q, kv, hg, r = _in_proj(x2, p['g_norm1'], sc1, sh1, p['w_in'], p['wkv_t'], consts['bd'], p['q_gain'], k_gain,
                            tm=tm, tiles_per_group=tiles_per_group, kv_t=kv_t)
    oa = attend(q, kv)

    if seq >= HGRN_CHUNK:
        ob3, s_new = _hgrn(hg.reshape(nseq, seq, 4 * W_B), s0, p['lb_logits'], p['hgrn_gain'],
                           layer=layer, chunk=HGRN_CHUNK, valid=HGRN_CHUNK, nseq=1, nchunks=4)
        ob = ob3.reshape(rows, W_B)
    else:
        padded = 8
        hg3 = jnp.pad(hg.reshape(nseq, seq, 4 * W_B), ((0, 0), (0, padded - seq), (0, 0)))
        ob3, s_new = _hgrn(hg3, s0, p['lb_logits'], p['hgrn_gain'],
                           layer=layer, chunk=padded, valid=seq, nseq=4, nchunks=1)
        ob = ob3[:, :seq].reshape(rows, W_B)

    x1 = _mix(oa, ob, r, x2, gt1, p['w_pa'], p['w_pb'], p['w_o'], tm=tm, tiles_per_group=tiles_per_group)

    d_ff = p['w_down'].shape[0]
    if seq >= tm:
        e1 = e2 = conv_prev
        ut_rows = 8
    else:
        z = jnp.zeros((nseq, seq - 2, d_ff), F32)
        e1 = jnp.concatenate([conv_prev[:, 1:2], jnp.zeros((nseq, seq - 1, d_ff), F32)], axis=1).reshape(rows, d_ff)
        e2 = jnp.concatenate([conv_prev, z], axis=1).reshape(rows, d_ff)
        ut_rows = tm
    y, ut = _ffn(x1, p['g_norm2'], sc2, sh2, gt2, p['w_up'], p['conv_w'], p['conv_b'], p['w_down'], e1, e2,
                 tm=tm, tiles_per_group=tiles_per_group, seq=seq, ut_rows=ut_rows)
    if seq >= tm:
        tps = seq // tm
        conv_new = ut.reshape(nseq, tps, 8, d_ff)[:, tps - 1, 6:8]
    else:
        conv_new = ut.reshape(nseq, seq, d_ff)[:, seq - 2:]
    if kv_t:
        k_out = kv[0].reshape(nseq, H_A, DH_A, seq).transpose(0, 3, 1, 2)
        v_out = kv[1].reshape(nseq, H_A, DH_A, seq).transpose(0, 3, 1, 2)
    else:
        k_out = kv[0].reshape(nseq, seq, H_A, DH_A)
        v_out = kv[1].reshape(nseq, seq, H_A, DH_A)
    return y.reshape(nseq, seq, d), k_out, v_out, s_new, conv_new


def kernel(x_prompt, x_sample, cache_k, cache_v, state_hgrn, state_conv, page_table, c_prompt, c_sample,
           w_ada, b_ada, g_norm1, w_in, q_gain, k_gain, sb_bias, hgrn_lb_logits, hgrn_gain, w_pa, w_pb, w_o,
           g_norm2, w_up, conv_w, conv_b, w_down):
    depth = w_ada.shape[0]
    n_pr, seq, d = x_prompt.shape
    n_dec, dec_seq, _ = x_sample.shape
    d_ff = w_down.shape[1]
    n_pool, page = cache_k.shape[1], cache_k.shape[2]

    mod_all = _ada(jnp.concatenate([c_prompt, c_sample], axis=0), w_ada, b_ada)

    head = np.arange(W_A) // DH_A
    consts = {'bd': jnp.asarray((head[:, None] == head[None, :]).astype(np.float32) / DH_A, dtype=BF16)}
    uo_prompt = _suffix_matrix(ATT_TK)
    uo_page = _suffix_matrix(page)
    ck = jnp.transpose(cache_k, (0, 1, 3, 4, 2)).reshape(depth, n_pool, W_A, page)
    cv = jnp.transpose(cache_v, (0, 1, 3, 4, 2)).reshape(depth, n_pool, W_A, page)
    rowh = np.arange(dec_seq * H_A) % H_A
    qmask = jnp.asarray((rowh[:, None] == head[None, :]), dtype=BF16)

    yp, ys = x_prompt, x_sample
    outs = [[] for _ in range(8)]
    for l in range(depth):
        p = {
            'g_norm1': g_norm1[l][None], 'g_norm2': g_norm2[l][None],
            'w_in': w_in[l].astype(BF16), 'w_pa': w_pa[l].astype(BF16), 'w_pb': w_pb[l].astype(BF16),
            'wkv_t': w_in[l][:, W_A:3 * W_A].T.reshape(2, W_A, d).astype(BF16),
            'w_o': w_o[l].astype(BF16), 'w_up': w_up[l].astype(BF16), 'w_down': w_down[l].astype(BF16),
            'q_gain': jnp.tile(q_gain[l], H_A)[None], 'k_gain': jnp.tile(k_gain[l], H_A)[None],
            'lb_logits': hgrn_lb_logits, 'hgrn_gain': hgrn_gain[l][None],
            'conv_w': conv_w[l], 'conv_b': conv_b[l][None],
        }

        def attend_prompt(q, kv, l=l):
            return _attn_prompt(q, kv, sb_bias[l], uo_prompt, batch=n_pr, seq=seq)

        def attend_sample(q, kv, l=l):
            q = q.reshape(n_dec, dec_seq, 1, W_A)
            qbd = (jnp.broadcast_to(q, (n_dec, dec_seq, H_A, W_A)).reshape(n_dec, dec_seq * H_A, W_A)
                   * qmask[None])
            bias_rows = jnp.broadcast_to(jnp.tile(sb_bias[l], dec_seq)[:, None], (dec_seq * H_A, V7X_LANES))
            knew = jnp.pad(kv[0].reshape(n_dec, dec_seq, W_A), ((0, 0), (0, 8 - dec_seq), (0, 0)))
            vnew = jnp.pad(kv[1].reshape(n_dec, dec_seq, W_A), ((0, 0), (0, 8 - dec_seq), (0, 0)))
            o = _attn_sample(page_table, qbd, bias_rows, knew, vnew, uo_page, ck, cv, layer=l)
            return o.reshape(n_dec * dec_seq, W_A).astype(BF16)

        yp, kp, vp, hp, cp = _layer(
            yp, mod_all[l, :n_pr], p, l, attend=attend_prompt,
            s0=jnp.zeros((n_pr, H_B, DK_B, DK_B), F32), conv_prev=jnp.zeros((n_pr, CONV_W - 1, d_ff), F32),
            tm=512, consts=consts)
        ys, ksm, vsm, hs, cs = _layer(
            ys, mod_all[l, n_pr:], p, l, attend=attend_sample,
            s0=state_hgrn[l], conv_prev=state_conv[l], tm=n_dec * dec_seq, consts=consts)
        for lst, val in zip(outs, (kp, vp, ksm, vsm, hp, hs, cp, cs)):
            lst.append(val)

    return (yp, ys) + tuple(jnp.stack(o) for o in outs)
```

```python
import functools

import numpy as np
import jax
import jax.numpy as jnp
from jax import lax
from jax.experimental import pallas as pl
from jax.experimental.pallas import tpu as pltpu

F32 = jnp.float32
BF16 = jnp.bfloat16

EPS = 1e-6
V7X_LANES = 128
V7X_VMEM_LIMIT_BYTES = 56 * 1024 * 1024

H_A = 8
DH_A = 64
W_A = H_A * DH_A
H_B = 4
DK_B = 128
W_B = H_B * DK_B
CONV_W = 3
HGRN_CHUNK = 64


def _cparams(sem):
    return pltpu.CompilerParams(dimension_semantics=sem, vmem_limit_bytes=V7X_VMEM_LIMIT_BYTES)


def _sigmoid(x):
    return 1.0 / (1.0 + jnp.exp(-x))


def _dot(a, b):
    return jnp.dot(a, b, preferred_element_type=F32)


def _dot_nt(a, b):
    return lax.dot_general(a, b, (((1,), (1,)), ((), ())), preferred_element_type=F32)


def _split2(x):
    hi = x.astype(BF16)
    lo = (x - hi.astype(F32)).astype(BF16)
    return hi, lo


def _ada_kernel(c_ref, w_ref, b_ref, o_ref):
    c = c_ref[...]
    s = c * _sigmoid(c)
    s_hi, s_lo = _split2(s)
    w_hi, w_lo = _split2(w_ref[...])
    acc = _dot(s_hi, w_hi) + _dot(s_hi, w_lo) + _dot(s_lo, w_hi)
    o_ref[...] = acc + b_ref[...]


def _ada(c_all, w_ada, b_ada):
    depth, d, n6 = w_ada.shape
    nc = c_all.shape[0]
    tn = 1536
    return pl.pallas_call(
        _ada_kernel,
        out_shape=jax.ShapeDtypeStruct((depth, nc, n6), F32),
        grid=(depth, n6 // tn),
        in_specs=[
            pl.BlockSpec((nc, d), lambda l, j: (0, 0)),
            pl.BlockSpec((None, d, tn), lambda l, j: (l, 0, j)),
            pl.BlockSpec((None, 1, tn), lambda l, j: (l, 0, j)),
        ],
        out_specs=pl.BlockSpec((None, nc, tn), lambda l, j: (l, 0, j)),
        compiler_params=_cparams(("parallel", "parallel")),
        name="ada_mod",
    )(c_all, w_ada, b_ada.reshape(depth, 1, n6))


IN_TN = 512


def _in_kernel(x_ref, g_ref, sc_ref, sh_ref, w_ref, wkv_ref, bd_ref, qg_ref, kg_ref,
               q_ref, k_ref, v_ref, hg_ref, r_ref, h_scr, *, kv_t):
    j = pl.program_id(1)

    @pl.when(j == 0)
    def _():
        x = x_ref[...]
        ms = jnp.mean(x * x, axis=-1, keepdims=True)
        y = x * lax.rsqrt(ms + EPS) * g_ref[...]
        h_scr[...] = (y * (1.0 + sc_ref[...]) + sh_ref[...]).astype(BF16)

    def proj():
        return _dot(h_scr[...], w_ref[...])

    def proj_t():
        return _dot_nt(wkv_ref[...], h_scr[...])

    def head_norm(t):
        ms = _dot((t * t).astype(BF16), bd_ref[...])
        return t * lax.rsqrt(ms + EPS)

    @pl.when(j == 0)
    def _():
        q_ref[...] = (head_norm(proj()) * qg_ref[...] * (DH_A ** -0.5)).astype(BF16)

    @pl.when(j == 1)
    def _():
        if kv_t:
            t = proj_t()
            t3 = t.reshape(H_A, DH_A, t.shape[1])
            ms = jnp.mean(t3 * t3, axis=1, keepdims=True)
            k_ref[...] = (t3 * lax.rsqrt(ms + EPS)).reshape(t.shape) * kg_ref[...]
        else:
            t = proj()
            sq_hi, sq_lo = _split2(t * t)
            ms = _dot(sq_hi, bd_ref[...]) + _dot(sq_lo, bd_ref[...])
            k_ref[...] = t * lax.rsqrt(ms + EPS) * kg_ref[...]

    @pl.when(j == 2)
    def _():
        v_ref[...] = proj_t() if kv_t else proj()

    @pl.when(jnp.logical_and(j >= 3, j <= 6))
    def _():
        hg_ref[...] = proj()

    @pl.when(j >= 7)
    def _():
        r_ref[...] = _sigmoid(proj()).astype(r_ref.dtype)


def _in_proj(x2, g, sc, sh, w_tiles, wkv_t, bd, qg, kg, *, tm, tiles_per_group, kv_t):
    rows, d = x2.shape
    nj = w_tiles.shape[0]
    rb = sc.shape[1]
    mod_spec = pl.BlockSpec((None, rb, d), lambda i, j: (i // tiles_per_group, 0, 0))
    kv_sel = lambda j: jnp.clip(j - 1, 0, 1)
    if kv_t:
        nseq = rows // (tiles_per_group * tm)
        kv_shape = (nseq, W_A, tiles_per_group * tm)
        kv_spec = pl.BlockSpec((None, W_A, tm), lambda i, j: (i // tiles_per_group, 0, i % tiles_per_group))
        w_spec = pl.BlockSpec((None, d, IN_TN), lambda i, j: (jnp.where((j == 1) | (j == 2), 0, j), 0, 0))
        kg_spec = pl.BlockSpec((W_A, 1), lambda i, j: (0, 0))
    else:
        kv_shape = (rows, W_A)
        kv_spec = pl.BlockSpec((tm, IN_TN), lambda i, j: (i, 0))
        w_spec = pl.BlockSpec((None, d, IN_TN), lambda i, j: (j, 0, 0))
        kg_spec = pl.BlockSpec((1, W_A), lambda i, j: (0, 0))
    return pl.pallas_call(
        functools.partial(_in_kernel, kv_t=kv_t),
        out_shape=(
            jax.ShapeDtypeStruct((rows, W_A), BF16),
            jax.ShapeDtypeStruct(kv_shape, F32),
            jax.ShapeDtypeStruct(kv_shape, F32),
            jax.ShapeDtypeStruct((4, rows, W_B), F32),
            jax.ShapeDtypeStruct((4, rows, IN_TN), BF16),
        ),
        grid=(rows // tm, nj),
        in_specs=[
            pl.BlockSpec((tm, d), lambda i, j: (i, 0)),
            pl.BlockSpec((1, d), lambda i, j: (0, 0)),
            mod_spec,
            mod_spec,
            w_spec,
            pl.BlockSpec((None, W_A, d), lambda i, j: (kv_sel(j), 0, 0)),
            pl.BlockSpec((W_A, W_A), lambda i, j: (0, 0)),
            pl.BlockSpec((1, W_A), lambda i, j: (0, 0)),
            kg_spec,
        ],
        out_specs=(
            pl.BlockSpec((tm, IN_TN), lambda i, j: (i, 0)),
            kv_spec,
            kv_spec,
            pl.BlockSpec((None, tm, IN_TN), lambda i, j: (jnp.clip(j - 3, 0, 3), i, 0)),
            pl.BlockSpec((None, tm, IN_TN), lambda i, j: (jnp.clip(j - 7, 0, 3), i, 0)),
        ),
        scratch_shapes=[pltpu.VMEM((tm, d), BF16)],
        compiler_params=_cparams(("parallel", "arbitrary")),
        name="in_proj",
    )(x2, g, sc, sh, w_tiles, wkv_t, bd, qg, kg)


def _sb_keep(z, mask):
    neg_abs = pltpu.bitcast(pltpu.bitcast(z, jnp.uint32) | jnp.uint32(0x80000000), F32)
    sp = jnp.maximum(z, 0.0) + jnp.log(1.0 + jnp.exp(neg_abs))
    spm = sp if mask is None else jnp.where(mask, sp, 0.0)
    return spm, z - sp


def _sb_weights(log_beta, after, c, mask):
    n = log_beta.shape[1]
    c_full = c if n == V7X_LANES else jnp.concatenate([c] * (n // V7X_LANES), axis=1)
    a = jnp.exp(log_beta + after + c_full)
    return a if mask is None else jnp.where(mask, a, 0.0)


def _sb_blocks(zs, c, neg_u, masks):
    keeps = [_sb_keep(z, m) for z, m in zip(zs, masks)]
    afters = [_dot(sp.astype(BF16), neg_u) for sp, _ in keeps]
    tots = [jnp.sum(sp, axis=1, keepdims=True) for sp, _ in keeps]
    ws = []
    for (_, log_beta), after, tot, m in zip(keeps, afters, tots, masks):
        ws.append(_sb_weights(log_beta, after, c, m).astype(BF16))
        c = c - tot
    return ws, c


def _suffix_matrix(n):
    j = np.arange(n)[:, None]
    s = np.arange(n)[None, :]
    return jnp.asarray(-(j > s).astype(np.float32), dtype=BF16)


ATT_TQ = 256
ATT_TK = 256


def _attn_prompt_kernel(bias_ref, q_ref, kt_ref, vt_ref, u_ref, o_ref, kv_scr, acc_scr, c_scr):
    hp = pl.program_id(1)
    qi = pl.program_id(2)
    tq, tk = ATT_TQ, ATT_TK
    nkb = kv_scr.shape[1]
    first_head = lax.broadcasted_iota(jnp.int32, (1, V7X_LANES), 1) < DH_A

    @pl.when(qi == 0)
    def _():
        for kb in range(nkb):
            kv_scr[0, kb] = kt_ref[:, kb * tk:(kb + 1) * tk].astype(BF16)
            kv_scr[1, kb] = vt_ref[:, kb * tk:(kb + 1) * tk].astype(BF16)

    q = q_ref[...]
    zero = jnp.zeros_like(q)
    q2 = jnp.concatenate([jnp.where(first_head, q, zero), jnp.where(first_head, zero, q)], axis=0)
    u = u_ref[...]
    b0 = bias_ref[2 * hp]
    b1 = bias_ref[2 * hp + 1]
    row = lax.broadcasted_iota(jnp.int32, (2 * tq, tk), 0) & (tq - 1)
    col = lax.broadcasted_iota(jnp.int32, (2 * tq, tk), 1)

    def steps(kbs, masks):
        zs = []
        for kb in kbs:
            s = _dot(q2, kv_scr[0, kb])
            zs.append(jnp.concatenate([s[:tq] + b0, s[tq:] + b1], axis=0))
        ws, c = _sb_blocks(zs, c_scr[...], u, masks)
        acc = acc_scr[...]
        for kb, w in zip(kbs, ws):
            acc = acc + _dot_nt(w, kv_scr[1, kb])
        acc_scr[...] = acc
        c_scr[...] = c

    acc_scr[...] = jnp.zeros_like(acc_scr)
    c_scr[...] = jnp.zeros_like(c_scr)
    steps([qi], [col < row])

    def body(i, carry):
        kb = qi - 1 - 4 * i
        steps([kb, kb - 1, kb - 2, kb - 3], [None] * 4)
        return carry

    lax.fori_loop(0, qi // 4, body, 0)
    rem = qi & 3

    @pl.when((rem & 2) != 0)
    def _():
        steps([rem - 1, rem - 2], [None] * 2)

    @pl.when((rem & 1) != 0)
    def _():
        steps([0], [None])

    o_ref[...] = jnp.where(first_head, acc_scr[:tq], acc_scr[tq:]).astype(o_ref.dtype)


def _attn_prompt(q, kt, vt, sb_bias, u, *, batch, seq):
    rows = q.shape[0]
    nq = seq // ATT_TQ
    npair = H_A // 2
    return pl.pallas_call(
        _attn_prompt_kernel,
        out_shape=jax.ShapeDtypeStruct((rows, W_A), BF16),
        grid=(batch, npair, nq),
        in_specs=[
            pl.BlockSpec(memory_space=pltpu.SMEM),
            pl.BlockSpec((ATT_TQ, V7X_LANES), lambda b, hp, qi: (b * nq + qi, hp)),
            pl.BlockSpec((None, V7X_LANES, seq), lambda b, hp, qi: (b, hp, 0)),
            pl.BlockSpec((None, V7X_LANES, seq), lambda b, hp, qi: (b, hp, 0)),
            pl.BlockSpec((ATT_TK, ATT_TK), lambda b, hp, qi: (0, 0)),
        ],
        out_specs=pl.BlockSpec((ATT_TQ, V7X_LANES), lambda b, hp, qi: (b * nq + qi, hp)),
        scratch_shapes=[
            pltpu.VMEM((2, seq // ATT_TK, V7X_LANES, ATT_TK), BF16),
            pltpu.VMEM((2 * ATT_TQ, V7X_LANES), F32),
            pltpu.VMEM((2 * ATT_TQ, V7X_LANES), F32),
        ],
        compiler_params=_cparams(("parallel", "parallel", "arbitrary")),
        name="attn_prompt",
    )(sb_bias, q, kt, vt, u)


PAGES_PER_STEP = 16


def _attn_sample_kernel(pt_ref, qbd_ref, bias_ref, knew_ref, vnew_ref, u_ref, *rest):
    npg = PAGES_PER_STEP
    k_refs = rest[:npg]
    v_refs = rest[npg:2 * npg]
    o_ref = rest[2 * npg]
    acc_scr, c_scr = rest[2 * npg + 1:]
    p = pl.program_id(1)
    nrow = qbd_ref.shape[0]
    dec_seq = nrow // H_A
    page = u_ref.shape[0]
    qbd = qbd_ref[...]
    bias = bias_ref[...]
    u = u_ref[...]

    @pl.when(p == 0)
    def _():
        pad = jnp.zeros((page - knew_ref.shape[0], W_A), F32)
        kn = jnp.concatenate([knew_ref[...], pad], axis=0).astype(BF16)
        vn = jnp.concatenate([vnew_ref[...], pad], axis=0).astype(BF16)
        t = lax.broadcasted_iota(jnp.int32, (nrow, page), 0) >> 3
        s = lax.broadcasted_iota(jnp.int32, (nrow, page), 1)
        ws, c = _sb_blocks([_dot_nt(qbd, kn) + bias], jnp.zeros((nrow, V7X_LANES), F32), u, [s < t])
        acc_scr[...] = _dot(ws[0], vn)
        c_scr[...] = c

    zs = [_dot(qbd, k_refs[i][...].astype(BF16)) + bias for i in range(npg)]
    ws, c = _sb_blocks(zs, c_scr[...], u, [None] * npg)
    acc = acc_scr[...]
    for i in range(npg):
        acc = acc + _dot_nt(ws[i], v_refs[i][...].astype(BF16))
    acc_scr[...] = acc
    c_scr[...] = c

    @pl.when(p == pl.num_programs(1) - 1)
    def _():
        r = lax.broadcasted_iota(jnp.int32, (nrow, W_A), 0)
        l = lax.broadcasted_iota(jnp.int32, (nrow, W_A), 1)
        own = (r & (H_A - 1)) == (l >> 6)
        om = jnp.where(own, acc_scr[...], 0.0)
        o_ref[...] = jnp.sum(om.reshape(dec_seq, H_A, W_A), axis=1)


def _attn_sample(page_table, qbd, bias_rows, knew, vnew, uo, cache_k, cache_v, *, layer):
    nb, nrow, _ = qbd.shape
    n_pages = page_table.shape[1]
    page = cache_k.shape[3]
    nsteps = n_pages // PAGES_PER_STEP
    dec_seq = nrow // H_A

    def page_spec(i):
        def imap(b, p, pt):
            return (layer, pt[b, n_pages - 1 - (p * PAGES_PER_STEP + i)], 0, 0)
        return pl.BlockSpec((None, None, W_A, page), imap)

    grid_spec = pltpu.PrefetchScalarGridSpec(
        num_scalar_prefetch=1,
        grid=(nb, nsteps),
        in_specs=[
            pl.BlockSpec((None, nrow, W_A), lambda b, p, pt: (b, 0, 0)),
            pl.BlockSpec((nrow, V7X_LANES), lambda b, p, pt: (0, 0)),
            pl.BlockSpec((None, 8, W_A), lambda b, p, pt: (b, 0, 0)),
            pl.BlockSpec((None, 8, W_A), lambda b, p, pt: (b, 0, 0)),
            pl.BlockSpec((page, page), lambda b, p, pt: (0, 0)),
        ] + [page_spec(i) for i in range(PAGES_PER_STEP)] + [page_spec(i) for i in range(PAGES_PER_STEP)],
        out_specs=pl.BlockSpec((None, dec_seq, W_A), lambda b, p, pt: (b, 0, 0)),
        scratch_shapes=[
            pltpu.VMEM((nrow, W_A), F32),
            pltpu.VMEM((nrow, V7X_LANES), F32),
        ],
    )
    return pl.pallas_call(
        _attn_sample_kernel,
        out_shape=jax.ShapeDtypeStruct((nb, dec_seq, W_A), F32),
        grid_spec=grid_spec,
        compiler_params=_cparams(("parallel", "arbitrary")),
        name="attn_sample",
    )(page_table, qbd, bias_rows, knew, vnew, uo,
      *([cache_k] * PAGES_PER_STEP), *([cache_v] * PAGES_PER_STEP))


def _seg_bcast(x, n, off):
    rows, lanes = x.shape
    if n >= 8:
        parts = [jnp.broadcast_to(x[b * n + off:b * n + off + 1, :], (n, lanes)) for b in range(rows // n)]
        return jnp.concatenate(parts, axis=0)
    sub = lax.broadcasted_iota(jnp.int32, (8, lanes), 0)
    parts = []
    for g in range(rows // 8):
        acc = jnp.broadcast_to(x[g * 8 + off:g * 8 + off + 1, :], (8, lanes))
        for sb in range(1, 8 // n):
            r = g * 8 + sb * n + off
            acc = jnp.where(sub >= sb * n, jnp.broadcast_to(x[r:r + 1, :], (8, lanes)), acc)
        parts.append(acc)
    return jnp.concatenate(parts, axis=0)


def _hgrn_levels(chunk):
    out = []
    n = chunk
    while n >= 2:
        out.append(n)
        n //= 2
    return out


def _hgrn_masks(chunk, groups):
    rows = chunk * groups
    t = np.arange(rows)
    m = [(t[:, None] // n == t[None, :] // n) for n in _hgrn_levels(chunk)]
    m.append(t[:, None] == t[None, :])
    return jnp.asarray(np.stack(m).astype(np.float32))


def _hgrn_kernel(lbl_ref, gain_ref, masks_ref, tri_ref, hg_ref, s0_ref, ob_ref, sout_ref, s_scr,
                 *, layer, chunk, valid, nseq, nchunks):
    j = pl.program_id(1)
    groups = nseq * H_B
    rows = groups * chunk
    levels = _hgrn_levels(chunk)

    @pl.when(j == 0)
    def _():
        s_scr[...] = s0_ref[...].reshape(groups, DK_B, DK_B)

    lg = lbl_ref[...]
    ex = jnp.exp(lg - jnp.max(lg, axis=0, keepdims=True))
    wts = ex / jnp.sum(ex, axis=0, keepdims=True)
    lb_row = jnp.sum(wts[:layer + 1], axis=0, keepdims=True) - wts[0:1]

    def stack(get):
        return jnp.concatenate([get(s, h) for s in range(nseq) for h in range(H_B)], axis=0)

    lbs = stack(lambda s, h: jnp.broadcast_to(lb_row[:, h * DK_B:(h + 1) * DK_B], (chunk, DK_B)))
    tloc = lax.broadcasted_iota(jnp.int32, (rows, DK_B), 0) & (chunk - 1)
    tri = tri_ref[...]
    gain = gain_ref[...]

    for ci in range(nchunks):
        r0 = ci * chunk

        def seg(col, r0=r0):
            return stack(lambda s, h: hg_ref[col, s, r0:r0 + chunk, h * DK_B:(h + 1) * DK_B])

        zq, zf, vi, zg = seg(0), seg(1), seg(2), seg(3)
        logf = jnp.log(lbs + (1.0 - lbs) * _sigmoid(zf))
        kk = (1.0 - lbs) * _sigmoid(-zf)
        qs = zq * _sigmoid(zq)
        if valid < chunk:
            ok = tloc < valid
            logf = jnp.where(ok, logf, 0.0)
            kk = jnp.where(ok, kk, 0.0)
            qs = jnp.where(ok, qs, 0.0)

        l_hi = logf.astype(BF16)
        rem = logf - l_hi.astype(F32)
        l_mid = rem.astype(BF16)
        l_lo = (rem - l_mid.astype(F32)).astype(BF16)
        cum = _dot(tri, l_hi) + _dot(tri, l_mid) + _dot(tri, l_lo)

        att = _dot_nt(qs.astype(BF16), kk.astype(BF16)) * masks_ref[len(levels)]
        for li, n in enumerate(levels):
            ref_row = _seg_bcast(cum, n, n // 2 - 1)
            e = jnp.exp(-jnp.abs(cum - ref_row))
            upper = (tloc & (n - 1)) >= (n // 2)
            qn = jnp.where(upper, qs * e, 0.0).astype(BF16)
            kn = jnp.where(upper, 0.0, kk * e).astype(BF16)
            att = att + _dot_nt(qn, kn) * masks_ref[li]

        o = _dot(att.astype(BF16), vi.astype(BF16))
        qe = qs * jnp.exp(cum)
        o = o + jnp.concatenate(
            [_dot(qe[g * chunk:(g + 1) * chunk].astype(BF16), s_scr[g].astype(BF16)) for g in range(groups)],
            axis=0)

        on = o * lax.rsqrt(jnp.mean(o * o, axis=-1, keepdims=True) + EPS) * gain
        og = on * (zg * _sigmoid(zg))
        for s in range(nseq):
            ob_ref[s, r0:r0 + chunk, :] = jnp.concatenate(
                [og[(s * H_B + h) * chunk:(s * H_B + h + 1) * chunk] for h in range(H_B)], axis=1)

        last = _seg_bcast(cum, chunk, chunk - 1)
        kd_t = (kk * jnp.exp(last - cum)).T
        dec_t = jnp.exp(last).T
        colg = lax.broadcasted_iota(jnp.int32, (DK_B, rows), 1) >> (chunk.bit_length() - 1)
        vb = vi.astype(BF16)
        for g in range(groups):
            kd_g = jnp.where(colg == g, kd_t, 0.0).astype(BF16)
            s_scr[g] = dec_t[:, g * chunk:g * chunk + 1] * s_scr[g] + _dot(kd_g, vb)

    @pl.when(j == pl.num_programs(1) - 1)
    def _():
        sout_ref[...] = s_scr[...].reshape(nseq, H_B, DK_B, DK_B)


def _hgrn(hg4, s0, lb_logits, gain, *, layer, chunk, valid, nseq, nchunks):
    _, nb, tp, _ = hg4.shape
    rb = chunk * nchunks
    groups = nseq * H_B
    rows = groups * chunk
    masks = _hgrn_masks(chunk, groups)
    t = np.arange(rows)
    tri = jnp.asarray(((t[:, None] // chunk == t[None, :] // chunk) & (t[None, :] <= t[:, None])).astype(np.float32),
                      dtype=BF16)
    kern = functools.partial(_hgrn_kernel, layer=layer, chunk=chunk, valid=valid, nseq=nseq, nchunks=nchunks)
    return pl.pallas_call(
        kern,
        out_shape=(
            jax.ShapeDtypeStruct((nb, tp, W_B), F32),
            jax.ShapeDtypeStruct((nb, H_B, DK_B, DK_B), F32),
        ),
        grid=(nb // nseq, tp // rb),
        in_specs=[
            pl.BlockSpec(lb_logits.shape, lambda b, j: (0, 0)),
            pl.BlockSpec((1, DK_B), lambda b, j: (0, 0)),
            pl.BlockSpec(masks.shape, lambda b, j: (0, 0, 0)),
            pl.BlockSpec((rows, rows), lambda b, j: (0, 0)),
            pl.BlockSpec((4, nseq, rb, W_B), lambda b, j: (0, b, j, 0)),
            pl.BlockSpec((nseq, H_B, DK_B, DK_B), lambda b, j: (b, 0, 0, 0)),
        ],
        out_specs=(
            pl.BlockSpec((nseq, rb, W_B), lambda b, j: (b, j, 0)),
            pl.BlockSpec((nseq, H_B, DK_B, DK_B), lambda b, j: (b, 0, 0, 0)),
        ),
        scratch_shapes=[pltpu.VMEM((groups, DK_B, DK_B), F32)],
        compiler_params=_cparams(("parallel", "arbitrary")),
        name="hgrn",
    )(lb_logits, gain, masks, tri, hg4, s0)


def _mix_kernel(oa_ref, ob_ref, r_ref, x_ref, gt_ref, wpa_ref, wpb_ref, wo_ref, out_ref):
    half = r_ref.shape[2]
    pa = _dot(oa_ref[...], wpa_ref[...])
    pb = _dot(ob_ref[...].astype(BF16), wpb_ref[...])
    merged = jnp.concatenate(
        [r_ref[c].astype(F32) * pa[:, c * half:(c + 1) * half]
         + r_ref[2 + c].astype(F32) * pb[:, c * half:(c + 1) * half] for c in range(2)], axis=1)
    y = _dot(merged.astype(BF16), wo_ref[...])
    out_ref[...] = x_ref[...] + gt_ref[...] * y


def _mix(oa, ob, r, x2, gt, wpa, wpb, wo, *, tm, tiles_per_group):
    rows, d = x2.shape
    rb = gt.shape[1]
    return pl.pallas_call(
        _mix_kernel,
        out_shape=jax.ShapeDtypeStruct((rows, d), F32),
        grid=(rows // tm,),
        in_specs=[
            pl.BlockSpec((tm, W_A), lambda i: (i, 0)),
            pl.BlockSpec((tm, W_B), lambda i: (i, 0)),
            pl.BlockSpec((4, tm, IN_TN), lambda i: (0, i, 0)),
            pl.BlockSpec((tm, d), lambda i: (i, 0)),
            pl.BlockSpec((None, rb, d), lambda i: (i // tiles_per_group, 0, 0)),
            pl.BlockSpec((W_A, d), lambda i: (0, 0)),
            pl.BlockSpec((W_B, d), lambda i: (0, 0)),
            pl.BlockSpec((d, d), lambda i: (0, 0)),
        ],
        out_specs=pl.BlockSpec((tm, d), lambda i: (i, 0)),
        compiler_params=_cparams(("parallel",)),
        name="mix",
    )(oa, ob, r, x2, gt, wpa, wpb, wo)


FFN_CHUNKS = 2
FFN_SUB = 256


def _gelu_tanh(x):
    return 0.5 * x * (1.0 + jnp.tanh(0.7978845608028654 * (x + 0.044715 * x * x * x)))


def _ffn_kernel(x_ref, g_ref, sc_ref, sh_ref, gt_ref, wu_ref, wv_ref, cw_ref, cb_ref, wd_ref, e1_ref, e2_ref,
                out_ref, ut_ref, h_scr, acc_scr, carry_scr, *, seq, tiles_per_seq):
    i = pl.program_id(0)
    c = pl.program_id(1)
    tm = x_ref.shape[0]
    tc = wu_ref.shape[1]

    @pl.when(c == 0)
    def _():
        x = x_ref[...]
        ms = jnp.mean(x * x, axis=-1, keepdims=True)
        y = x * lax.rsqrt(ms + EPS) * g_ref[...]
        h_scr[...] = (y * (1.0 + sc_ref[...]) + sh_ref[...]).astype(BF16)
        acc_scr[...] = jnp.zeros_like(acc_scr)

    h = h_scr[...]
    col0 = pl.multiple_of(c * tc, tc)
    if tiles_per_seq >= 1:
        @pl.when((i % tiles_per_seq) == 0)
        def _():
            carry_scr[:, pl.ds(col0, tc)] = e1_ref[...]

    acts = []
    for a in range(0, tc, FFN_SUB):
        b = min(a + FFN_SUB, tc)
        w = b - a
        u = _dot(h, wu_ref[:, a:b])
        v = _dot(h, wv_ref[:, a:b])
        row = lax.broadcasted_iota(jnp.int32, (tm, w), 0)
        if tiles_per_seq >= 1:
            cols = pl.ds(pl.multiple_of(col0 + a, V7X_LANES), w)
            prev = carry_scr[:, cols]
            m1 = jnp.where(row >= 1, pltpu.roll(u, 1, 0), jnp.broadcast_to(prev[1:2], (tm, w)))
            m2 = jnp.where(row >= 2, pltpu.roll(u, 2, 0),
                           jnp.where(row == 1, jnp.broadcast_to(prev[1:2], (tm, w)),
                                     jnp.broadcast_to(prev[0:1], (tm, w))))
            carry_scr[:, cols] = u[tm - 2:tm, :]
        else:
            t = row & (seq - 1)
            m1 = jnp.where(t >= 1, pltpu.roll(u, 1, 0), e1_ref[:, a:b])
            m2 = jnp.where(t >= 2, pltpu.roll(u, 2, 0), e2_ref[:, a:b])
        conv = cb_ref[:, a:b] + m2 * cw_ref[0:1, a:b] + m1 * cw_ref[1:2, a:b] + u * cw_ref[2:3, a:b]
        acts.append((_gelu_tanh(conv) * v).astype(BF16))
        ut_ref[:, a:b] = u[tm - ut_ref.shape[0]:, :]
    acc_scr[...] = acc_scr[...] + _dot(jnp.concatenate(acts, axis=1), wd_ref[...])

    @pl.when(c == pl.num_programs(1) - 1)
    def _():
        out_ref[...] = x_ref[...] + gt_ref[...] * acc_scr[...]


def _ffn(x2, g, sc, sh, gt, wup, cw, cb, wd, e1, e2, *, tm, tiles_per_group, seq, ut_rows):
    rows, d = x2.shape
    d_ff = wd.shape[0]
    nc = FFN_CHUNKS
    tc = d_ff // nc
    assert tc * nc == d_ff and tc % V7X_LANES == 0
    rb = sc.shape[1]
    tiles_per_seq = seq // tm
    mod_spec = pl.BlockSpec((None, rb, d), lambda i, c: (i // tiles_per_group, 0, 0))
    if tiles_per_seq >= 1:
        e_spec = pl.BlockSpec((None, 2, tc), lambda i, c: (i // tiles_per_seq, 0, c))
    else:
        e_spec = pl.BlockSpec((tm, tc), lambda i, c: (i, c))
    kern = functools.partial(_ffn_kernel, seq=seq, tiles_per_seq=tiles_per_seq)
    return pl.pallas_call(
        kern,
        out_shape=(
            jax.ShapeDtypeStruct((rows, d), F32),
            jax.ShapeDtypeStruct((rows // tm, ut_rows, d_ff), F32),
        ),
        grid=(rows // tm, nc),
        in_specs=[
            pl.BlockSpec((tm, d), lambda i, c: (i, 0)),
            pl.BlockSpec((1, d), lambda i, c: (0, 0)),
            mod_spec, mod_spec, mod_spec,
            pl.BlockSpec((d, tc), lambda i, c: (0, c)),
            pl.BlockSpec((d, tc), lambda i, c: (0, nc + c)),
            pl.BlockSpec((CONV_W, tc), lambda i, c: (0, c)),
            pl.BlockSpec((1, tc), lambda i, c: (0, c)),
            pl.BlockSpec((tc, d), lambda i, c: (c, 0)),
            e_spec, e_spec,
        ],
        out_specs=(
            pl.BlockSpec((tm, d), lambda i, c: (i, 0)),
            pl.BlockSpec((None, ut_rows, tc), lambda i, c: (i, 0, c)),
        ),
        scratch_shapes=[
            pltpu.VMEM((tm, d), BF16),
            pltpu.VMEM((tm, d), F32),
            pltpu.VMEM((2, d_ff), F32),
        ],
        compiler_params=_cparams(("arbitrary", "arbitrary")),
        name="ffn",
    )(x2, g, sc, sh, gt, wup, wup, cw, cb, wd, e1, e2)


def _group_mods(mod, d, rows_per_seq, tile_rows):
    parts = [mod[:, k * d:(k + 1) * d] for k in range(6)]
    if rows_per_seq >= tile_rows:
        return [p[:, None, :] for p in parts]
    return [jnp.repeat(p, rows_per_seq, axis=0)[None] for p in parts]


def _layer(x3, mod, p, layer, *, attend, s0, conv_prev, tm, tm_in, consts):
    nseq, seq, d = x3.shape
    rows = nseq * seq
    x2 = x3.reshape(rows, d)
    tiles_per_group = max(seq // tm, 1) if seq >= tm else rows // tm
    tiles_per_group_in = max(seq // tm_in, 1) if seq >= tm_in else rows // tm_in
    sh1, sc1, gt1, sh2, sc2, gt2 = _group_mods(mod, d, seq, tm)

    kv_t = seq >= tm_in
    k_gain = p['k_gain'].reshape(W_A, 1) if kv_t else p['k_gain']
    q, k, v, hg, r = _in_proj(x2, p['g_norm1'], sc1, sh1, p['w_in'], p['wkv_t'], consts['bd'], p['q_gain'], k_gain,
                              tm=tm_in, tiles_per_group=tiles_per_group_in, kv_t=kv_t)
    oa = attend(q, k, v)

    hg4 = hg.reshape(4, nseq, seq, W_B)
    if seq >= HGRN_CHUNK:
        ob3, s_new = _hgrn(hg4, s0, p['lb_logits'], p['hgrn_gain'],
                           layer=layer, chunk=HGRN_CHUNK, valid=HGRN_CHUNK, nseq=1, nchunks=4)
        ob = ob3.reshape(rows, W_B)
    else:
        padded = 8
        hg4 = jnp.pad(hg4, ((0, 0), (0, 0), (0, padded - seq), (0, 0)))
        ob3, s_new = _hgrn(hg4, s0, p['lb_logits'], p['hgrn_gain'],
                           layer=layer, chunk=padded, valid=seq, nseq=4, nchunks=1)
        ob = ob3[:, :seq].reshape(rows, W_B)

    x1 = _mix(oa, ob, r, x2, gt1, p['w_pa'], p['w_pb'], p['w_o'], tm=tm, tiles_per_group=tiles_per_group)

    d_ff = p['w_down'].shape[0]
    if seq >= tm:
        e1 = e2 = conv_prev
        ut_rows = 8
    else:
        z = jnp.zeros((nseq, seq - 2, d_ff), F32)
        e1 = jnp.concatenate([conv_prev[:, 1:2], jnp.zeros((nseq, seq - 1, d_ff), F32)], axis=1).reshape(rows, d_ff)
        e2 = jnp.concatenate([conv_prev, z], axis=1).reshape(rows, d_ff)
        ut_rows = tm
    y, ut = _ffn(x1, p['g_norm2'], sc2, sh2, gt2, p['w_up'], p['conv_w'], p['conv_b'], p['w_down'], e1, e2,
                 tm=tm, tiles_per_group=tiles_per_group, seq=seq, ut_rows=ut_rows)
    if seq >= tm:
        tps = seq // tm
        conv_new = ut.reshape(nseq, tps, 8, d_ff)[:, tps - 1, 6:8]
    else:
        conv_new = ut.reshape(nseq, seq, d_ff)[:, seq - 2:]
    if kv_t:
        k_out = k.reshape(nseq, H_A, DH_A, seq).transpose(0, 3, 1, 2)
        v_out = v.reshape(nseq, H_A, DH_A, seq).transpose(0, 3, 1, 2)
    else:
        k_out = k.reshape(nseq, seq, H_A, DH_A)
        v_out = v.reshape(nseq, seq, H_A, DH_A)
    return y.reshape(nseq, seq, d), k_out, v_out, s_new, conv_new


def kernel(x_prompt, x_sample, cache_k, cache_v, state_hgrn, state_conv, page_table, c_prompt, c_sample,
           w_ada, b_ada, g_norm1, w_in, q_gain, k_gain, sb_bias, hgrn_lb_logits, hgrn_gain, w_pa, w_pb, w_o,
           g_norm2, w_up, conv_w, conv_b, w_down):
    depth = w_ada.shape[0]
    n_pr, seq, d = x_prompt.shape
    n_dec, dec_seq, _ = x_sample.shape
    d_ff = w_down.shape[1]
    n_pool, page = cache_k.shape[1], cache_k.shape[2]

    mod_all = _ada(jnp.concatenate([c_prompt, c_sample], axis=0), w_ada, b_ada)

    head = np.arange(W_A) // DH_A
    consts = {'bd': jnp.asarray((head[:, None] == head[None, :]).astype(np.float32) / DH_A, dtype=BF16)}
    uo_prompt = _suffix_matrix(ATT_TK)
    uo_page = _suffix_matrix(page)
    ck = jnp.transpose(cache_k, (0, 1, 3, 4, 2)).reshape(depth, n_pool, W_A, page)
    cv = jnp.transpose(cache_v, (0, 1, 3, 4, 2)).reshape(depth, n_pool, W_A, page)
    rowh = np.arange(dec_seq * H_A) % H_A
    qmask = jnp.asarray((rowh[:, None] == head[None, :]), dtype=BF16)

    yp, ys = x_prompt, x_sample
    outs = [[] for _ in range(8)]
    for l in range(depth):
        p = {
            'g_norm1': g_norm1[l][None], 'g_norm2': g_norm2[l][None],
            'w_in': w_in[l].astype(BF16).reshape(d, -1, IN_TN).transpose(1, 0, 2),
            'w_pa': w_pa[l].astype(BF16), 'w_pb': w_pb[l].astype(BF16),
            'wkv_t': w_in[l][:, W_A:3 * W_A].T.reshape(2, W_A, d).astype(BF16),
            'w_o': w_o[l].astype(BF16), 'w_up': w_up[l].astype(BF16), 'w_down': w_down[l].astype(BF16),
            'q_gain': jnp.tile(q_gain[l], H_A)[None], 'k_gain': jnp.tile(k_gain[l], H_A)[None],
            'lb_logits': hgrn_lb_logits, 'hgrn_gain': hgrn_gain[l][None],
            'conv_w': conv_w[l], 'conv_b': conv_b[l][None],
        }

        def attend_prompt(q, k, v, l=l):
            return _attn_prompt(q, k, v, sb_bias[l], uo_prompt, batch=n_pr, seq=seq)

        def attend_sample(q, k, v, l=l):
            q = q.reshape(n_dec, dec_seq, 1, W_A)
            qbd = (jnp.broadcast_to(q, (n_dec, dec_seq, H_A, W_A)).reshape(n_dec, dec_seq * H_A, W_A)
                   * qmask[None])
            bias_rows = jnp.broadcast_to(jnp.tile(sb_bias[l], dec_seq)[:, None], (dec_seq * H_A, V7X_LANES))
            knew = jnp.pad(k.reshape(n_dec, dec_seq, W_A), ((0, 0), (0, 8 - dec_seq), (0, 0)))
            vnew = jnp.pad(v.reshape(n_dec, dec_seq, W_A), ((0, 0), (0, 8 - dec_seq), (0, 0)))
            o = _attn_sample(page_table, qbd, bias_rows, knew, vnew, uo_page, ck, cv, layer=l)
            return o.reshape(n_dec * dec_seq, W_A).astype(BF16)

        yp, kp, vp, hp, cp = _layer(
            yp, mod_all[l, :n_pr], p, l, attend=attend_prompt,
            s0=jnp.zeros((n_pr, H_B, DK_B, DK_B), F32), conv_prev=jnp.zeros((n_pr, CONV_W - 1, d_ff), F32),
            tm=512, tm_in=min(1024, seq), consts=consts)
        ys, ksm, vsm, hs, cs = _layer(
            ys, mod_all[l, n_pr:], p, l, attend=attend_sample,
            s0=state_hgrn[l], conv_prev=state_conv[l], tm=n_dec * dec_seq, tm_in=n_dec * dec_seq, consts=consts)
        for lst, val in zip(outs, (kp, vp, ksm, vsm, hp, hs, cp, cs)):
            lst.append(val)

    return (yp, ys) + tuple(jnp.stack(o) for o in outs)
```

```python
import functools

import numpy as np
import jax
import jax.numpy as jnp
from jax import lax
from jax.experimental import pallas as pl
from jax.experimental.pallas import tpu as pltpu

F32 = jnp.float32
BF16 = jnp.bfloat16

EPS = 1e-6
V7X_LANES = 128
V7X_VMEM_LIMIT_BYTES = 56 * 1024 * 1024

H_A = 8
DH_A = 64
W_A = H_A * DH_A
H_B = 4
DK_B = 128
W_B = H_B * DK_B
CONV_W = 3
HGRN_CHUNK = 64


def _cparams(sem):
    return pltpu.CompilerParams(dimension_semantics=sem, vmem_limit_bytes=V7X_VMEM_LIMIT_BYTES)


def _sigmoid(x):
    return 1.0 / (1.0 + jnp.exp(-x))


def _dot(a, b):
    return jnp.dot(a, b, preferred_element_type=F32)


def _dot_nt(a, b):
    return lax.dot_general(a, b, (((1,), (1,)), ((), ())), preferred_element_type=F32)


def _split2(x):
    hi = x.astype(BF16)
    lo = (x - hi.astype(F32)).astype(BF16)
    return hi, lo


def _ada_kernel(c_ref, w_ref, b_ref, o_ref):
    c = c_ref[...]
    s = c * _sigmoid(c)
    s_hi, s_lo = _split2(s)
    w_hi, w_lo = _split2(w_ref[...])
    acc = _dot(s_hi, w_hi) + _dot(s_hi, w_lo) + _dot(s_lo, w_hi)
    o_ref[...] = acc + b_ref[...]


def _ada(c_all, w_ada, b_ada):
    depth, d, n6 = w_ada.shape
    nc = c_all.shape[0]
    tn = 1536
    return pl.pallas_call(
        _ada_kernel,
        out_shape=jax.ShapeDtypeStruct((depth, nc, n6), F32),
        grid=(depth, n6 // tn),
        in_specs=[
            pl.BlockSpec((nc, d), lambda l, j: (0, 0)),
            pl.BlockSpec((None, d, tn), lambda l, j: (l, 0, j)),
            pl.BlockSpec((None, 1, tn), lambda l, j: (l, 0, j)),
        ],
        out_specs=pl.BlockSpec((None, nc, tn), lambda l, j: (l, 0, j)),
        compiler_params=_cparams(("parallel", "parallel")),
        name="ada_mod",
    )(c_all, w_ada, b_ada.reshape(depth, 1, n6))


IN_TN = 512


def _in_kernel(x_ref, g_ref, sc_ref, sh_ref, w_ref, wkv_ref, bd_ref, qg_ref, kg_ref,
               q_ref, k_ref, v_ref, hg_ref, r_ref, h_scr, *, kv_t):
    j = pl.program_id(1)

    @pl.when(j == 0)
    def _():
        x = x_ref[...]
        ms = jnp.mean(x * x, axis=-1, keepdims=True)
        y = x * lax.rsqrt(ms + EPS) * g_ref[...]
        h_scr[...] = (y * (1.0 + sc_ref[...]) + sh_ref[...]).astype(BF16)

    def proj():
        return _dot(h_scr[...], w_ref[...])

    def proj_t():
        return _dot_nt(wkv_ref[...], h_scr[...])

    def head_norm(t):
        ms = _dot((t * t).astype(BF16), bd_ref[...])
        return t * lax.rsqrt(ms + EPS)

    @pl.when(j == 0)
    def _():
        q_ref[...] = (head_norm(proj()) * qg_ref[...] * (DH_A ** -0.5)).astype(BF16)

    @pl.when(j == 1)
    def _():
        if kv_t:
            t = proj_t()
            t3 = t.reshape(H_A, DH_A, t.shape[1])
            ms = jnp.mean(t3 * t3, axis=1, keepdims=True)
            k_ref[...] = (t3 * lax.rsqrt(ms + EPS)).reshape(t.shape) * kg_ref[...]
        else:
            t = proj()
            sq_hi, sq_lo = _split2(t * t)
            ms = _dot(sq_hi, bd_ref[...]) + _dot(sq_lo, bd_ref[...])
            k_ref[...] = t * lax.rsqrt(ms + EPS) * kg_ref[...]

    @pl.when(j == 2)
    def _():
        v_ref[...] = proj_t() if kv_t else proj()

    @pl.when(jnp.logical_and(j >= 3, j <= 6))
    def _():
        hg_ref[...] = proj()

    @pl.when(j >= 7)
    def _():
        r_ref[...] = _sigmoid(proj()).astype(r_ref.dtype)


def _in_proj(x2, g, sc, sh, w_tiles, wkv_t, bd, qg, kg, *, tm, tiles_per_group, kv_t):
    rows, d = x2.shape
    nj = w_tiles.shape[0]
    rb = sc.shape[1]
    mod_spec = pl.BlockSpec((None, rb, d), lambda i, j: (i // tiles_per_group, 0, 0))
    kv_sel = lambda j: jnp.clip(j - 1, 0, 1)
    if kv_t:
        nseq = rows // (tiles_per_group * tm)
        kv_shape = (nseq, W_A, tiles_per_group * tm)
        kv_spec = pl.BlockSpec((None, W_A, tm), lambda i, j: (i // tiles_per_group, 0, i % tiles_per_group))
        w_spec = pl.BlockSpec((None, d, IN_TN), lambda i, j: (jnp.where((j == 1) | (j == 2), 0, j), 0, 0))
        kg_spec = pl.BlockSpec((W_A, 1), lambda i, j: (0, 0))
    else:
        kv_shape = (rows, W_A)
        kv_spec = pl.BlockSpec((tm, IN_TN), lambda i, j: (i, 0))
        w_spec = pl.BlockSpec((None, d, IN_TN), lambda i, j: (j, 0, 0))
        kg_spec = pl.BlockSpec((1, W_A), lambda i, j: (0, 0))
    return pl.pallas_call(
        functools.partial(_in_kernel, kv_t=kv_t),
        out_shape=(
            jax.ShapeDtypeStruct((rows, W_A), BF16),
            jax.ShapeDtypeStruct(kv_shape, F32),
            jax.ShapeDtypeStruct(kv_shape, F32),
            jax.ShapeDtypeStruct((4, rows, W_B), F32),
            jax.ShapeDtypeStruct((4, rows, IN_TN), BF16),
        ),
        grid=(rows // tm, nj),
        in_specs=[
            pl.BlockSpec((tm, d), lambda i, j: (i, 0)),
            pl.BlockSpec((1, d), lambda i, j: (0, 0)),
            mod_spec,
            mod_spec,
            w_spec,
            pl.BlockSpec((None, W_A, d), lambda i, j: (kv_sel(j), 0, 0)),
            pl.BlockSpec((W_A, W_A), lambda i, j: (0, 0)),
            pl.BlockSpec((1, W_A), lambda i, j: (0, 0)),
            kg_spec,
        ],
        out_specs=(
            pl.BlockSpec((tm, IN_TN), lambda i, j: (i, 0)),
            kv_spec,
            kv_spec,
            pl.BlockSpec((None, tm, IN_TN), lambda i, j: (jnp.clip(j - 3, 0, 3), i, 0)),
            pl.BlockSpec((None, tm, IN_TN), lambda i, j: (jnp.clip(j - 7, 0, 3), i, 0)),
        ),
        scratch_shapes=[pltpu.VMEM((tm, d), BF16)],
        compiler_params=_cparams(("parallel", "arbitrary")),
        name="in_proj",
    )(x2, g, sc, sh, w_tiles, wkv_t, bd, qg, kg)


def _sb_keep(z, mask):
    sp = jnp.maximum(z, 0.0) + jnp.log(1.0 + jnp.exp(-jnp.abs(z)))
    spm = sp if mask is None else jnp.where(mask, sp, 0.0)
    return spm, z - sp


def _sb_weights(log_beta, after, c, mask):
    n = log_beta.shape[1]
    c_full = c if n == V7X_LANES else jnp.concatenate([c] * (n // V7X_LANES), axis=1)
    a = jnp.exp(log_beta + after + c_full)
    return a if mask is None else jnp.where(mask, a, 0.0)


def _sb_blocks(zs, c, neg_u, masks):
    keeps = [_sb_keep(z, m) for z, m in zip(zs, masks)]
    afters = [_dot(sp.astype(BF16), neg_u) for sp, _ in keeps]
    tots = [jnp.sum(sp, axis=1, keepdims=True) for sp, _ in keeps]
    ws = []
    for (_, log_beta), after, tot, m in zip(keeps, afters, tots, masks):
        ws.append(_sb_weights(log_beta, after, c, m).astype(BF16))
        c = c - tot
    return ws, c


def _suffix_matrix(n):
    j = np.arange(n)[:, None]
    s = np.arange(n)[None, :]
    return jnp.asarray(-(j > s).astype(np.float32), dtype=BF16)


ATT_TQ = 256
ATT_TK = 256


def _attn_kernel(pt_ref, bias_ref, q_ref, kt_ref, vt_ref, u_ref, qbd_ref, sbias_ref, knew_ref, vnew_ref, up_ref,
                 *rest, npg):
    k_refs = rest[:npg]
    v_refs = rest[npg:2 * npg]
    o_ref, os_ref, kv_scr, acc_scr, c_scr, sacc_scr, sc_scr = rest[2 * npg:]
    hp = pl.program_id(1)
    qi = pl.program_id(2)
    tq, tk = ATT_TQ, ATT_TK
    nkb = kv_scr.shape[1]
    first_head = lax.broadcasted_iota(jnp.int32, (1, V7X_LANES), 1) < DH_A
    nrow = qbd_ref.shape[0]
    dec_seq = nrow // H_A
    page = up_ref.shape[0]
    qbd = qbd_ref[...]
    sbias = sbias_ref[...]
    up = up_ref[...]

    @pl.when(qi == 0)
    def _():
        for kb in range(nkb):
            kv_scr[0, kb] = kt_ref[:, kb * tk:(kb + 1) * tk].astype(BF16)
            kv_scr[1, kb] = vt_ref[:, kb * tk:(kb + 1) * tk].astype(BF16)
        pad = jnp.zeros((page - knew_ref.shape[0], W_A), F32)
        kn = jnp.concatenate([knew_ref[...], pad], axis=0).astype(BF16)
        vn = jnp.concatenate([vnew_ref[...], pad], axis=0).astype(BF16)
        t = lax.broadcasted_iota(jnp.int32, (nrow, page), 0) >> 3
        s = lax.broadcasted_iota(jnp.int32, (nrow, page), 1)
        ws, c = _sb_blocks([_dot_nt(qbd, kn) + sbias], jnp.zeros((nrow, V7X_LANES), F32), up, [s < t])
        sacc_scr[...] = _dot(ws[0], vn)
        sc_scr[...] = c

    zs = [_dot(qbd, k_refs[i][...].astype(BF16)) + sbias for i in range(npg)]
    ws, c = _sb_blocks(zs, sc_scr[...], up, [None] * npg)
    sacc = sacc_scr[...]
    for i in range(npg):
        sacc = sacc + _dot_nt(ws[i], v_refs[i][...].astype(BF16))
    sacc_scr[...] = sacc
    sc_scr[...] = c

    q = q_ref[...]
    zero = jnp.zeros_like(q)
    q2 = jnp.concatenate([jnp.where(first_head, q, zero), jnp.where(first_head, zero, q)], axis=0)
    u = u_ref[...]
    b0 = bias_ref[2 * hp]
    b1 = bias_ref[2 * hp + 1]
    row = lax.broadcasted_iota(jnp.int32, (2 * tq, tk), 0) & (tq - 1)
    col = lax.broadcasted_iota(jnp.int32, (2 * tq, tk), 1)

    def steps(kbs, masks):
        zs = []
        for kb in kbs:
            s = _dot(q2, kv_scr[0, kb])
            zs.append(jnp.concatenate([s[:tq] + b0, s[tq:] + b1], axis=0))
        ws, c = _sb_blocks(zs, c_scr[...], u, masks)
        acc = acc_scr[...]
        for kb, w in zip(kbs, ws):
            acc = acc + _dot_nt(w, kv_scr[1, kb])
        acc_scr[...] = acc
        c_scr[...] = c

    acc_scr[...] = jnp.zeros_like(acc_scr)
    c_scr[...] = jnp.zeros_like(c_scr)
    steps([qi], [col < row])

    def body(i, carry):
        kb = qi - 1 - 4 * i
        steps([kb, kb - 1, kb - 2, kb - 3], [None] * 4)
        return carry

    lax.fori_loop(0, qi // 4, body, 0)
    rem = qi & 3

    @pl.when((rem & 2) != 0)
    def _():
        steps([rem - 1, rem - 2], [None] * 2)

    @pl.when((rem & 1) != 0)
    def _():
        steps([0], [None])

    o_ref[...] = jnp.where(first_head, acc_scr[:tq], acc_scr[tq:]).astype(o_ref.dtype)

    @pl.when(qi == pl.num_programs(2) - 1)
    def _():
        r = lax.broadcasted_iota(jnp.int32, (nrow, W_A), 0)
        l = lax.broadcasted_iota(jnp.int32, (nrow, W_A), 1)
        own = (r & (H_A - 1)) == (l >> 6)
        om = jnp.where(own, sacc_scr[...], 0.0)
        os_ref[...] = jnp.sum(om.reshape(dec_seq, H_A, W_A), axis=1)


def _attention(q, kt, vt, sb_bias, u, page_table, qbd, bias_rows, knew, vnew, u_page, cache_k, cache_v,
               *, batch, seq, layer):
    rows = q.shape[0]
    nq = seq // ATT_TQ
    npair = H_A // 2
    nb, nrow, _ = qbd.shape
    n_pages = page_table.shape[1]
    page = cache_k.shape[3]
    dec_seq = nrow // H_A
    assert nb == batch * npair and n_pages % nq == 0
    npg = n_pages // nq

    def page_spec(i):
        def imap(b, hp, qi, pt):
            return (layer, pt[b * npair + hp, n_pages - 1 - (qi * npg + i)], 0, 0)
        return pl.BlockSpec((None, None, W_A, page), imap)

    sample_seq = lambda b, hp, qi, pt: (b * npair + hp, 0, 0)
    grid_spec = pltpu.PrefetchScalarGridSpec(
        num_scalar_prefetch=1,
        grid=(batch, npair, nq),
        in_specs=[
            pl.BlockSpec(memory_space=pltpu.SMEM),
            pl.BlockSpec((ATT_TQ, V7X_LANES), lambda b, hp, qi, pt: (b * nq + qi, hp)),
            pl.BlockSpec((None, V7X_LANES, seq), lambda b, hp, qi, pt: (b, hp, 0)),
            pl.BlockSpec((None, V7X_LANES, seq), lambda b, hp, qi, pt: (b, hp, 0)),
            pl.BlockSpec((ATT_TK, ATT_TK), lambda b, hp, qi, pt: (0, 0)),
            pl.BlockSpec((None, nrow, W_A), sample_seq),
            pl.BlockSpec((nrow, V7X_LANES), lambda b, hp, qi, pt: (0, 0)),
            pl.BlockSpec((None, 8, W_A), sample_seq),
            pl.BlockSpec((None, 8, W_A), sample_seq),
            pl.BlockSpec((page, page), lambda b, hp, qi, pt: (0, 0)),
        ] + [page_spec(i) for i in range(npg)] + [page_spec(i) for i in range(npg)],
        out_specs=(
            pl.BlockSpec((ATT_TQ, V7X_LANES), lambda b, hp, qi, pt: (b * nq + qi, hp)),
            pl.BlockSpec((None, dec_seq, W_A), sample_seq),
        ),
        scratch_shapes=[
            pltpu.VMEM((2, seq // ATT_TK, V7X_LANES, ATT_TK), BF16),
            pltpu.VMEM((2 * ATT_TQ, V7X_LANES), F32),
            pltpu.VMEM((2 * ATT_TQ, V7X_LANES), F32),
            pltpu.VMEM((nrow, W_A), F32),
            pltpu.VMEM((nrow, V7X_LANES), F32),
        ],
    )
    return pl.pallas_call(
        functools.partial(_attn_kernel, npg=npg),
        out_shape=(
            jax.ShapeDtypeStruct((rows, W_A), BF16),
            jax.ShapeDtypeStruct((nb, dec_seq, W_A), F32),
        ),
        grid_spec=grid_spec,
        compiler_params=_cparams(("parallel", "parallel", "arbitrary")),
        name="attention",
    )(page_table, sb_bias, q, kt, vt, u, qbd, bias_rows, knew, vnew, u_page,
      *([cache_k] * npg), *([cache_v] * npg))


def _seg_bcast(x, n, off):
    rows, lanes = x.shape
    if n >= 8:
        parts = [jnp.broadcast_to(x[b * n + off:b * n + off + 1, :], (n, lanes)) for b in range(rows // n)]
        return jnp.concatenate(parts, axis=0)
    sub = lax.broadcasted_iota(jnp.int32, (8, lanes), 0)
    parts = []
    for g in range(rows // 8):
        acc = jnp.broadcast_to(x[g * 8 + off:g * 8 + off + 1, :], (8, lanes))
        for sb in range(1, 8 // n):
            r = g * 8 + sb * n + off
            acc = jnp.where(sub >= sb * n, jnp.broadcast_to(x[r:r + 1, :], (8, lanes)), acc)
        parts.append(acc)
    return jnp.concatenate(parts, axis=0)


def _hgrn_levels(chunk):
    out = []
    n = chunk
    while n >= 2:
        out.append(n)
        n //= 2
    return out


def _hgrn_masks(chunk, groups):
    rows = chunk * groups
    t = np.arange(rows)
    m = [(t[:, None] // n == t[None, :] // n) for n in _hgrn_levels(chunk)]
    m.append(t[:, None] == t[None, :])
    return jnp.asarray(np.stack(m).astype(np.float32))


def _hgrn_kernel(lbl_ref, gain_ref, masks_ref, tri_ref, hg_ref, s0_ref, ob_ref, sout_ref, s_scr,
                 *, layer, chunk, valid, nseq, nchunks):
    j = pl.program_id(1)
    groups = nseq * H_B
    rows = groups * chunk
    levels = _hgrn_levels(chunk)

    @pl.when(j == 0)
    def _():
        s_scr[...] = s0_ref[...].reshape(groups, DK_B, DK_B)

    lg = lbl_ref[...]
    ex = jnp.exp(lg - jnp.max(lg, axis=0, keepdims=True))
    wts = ex / jnp.sum(ex, axis=0, keepdims=True)
    lb_row = jnp.sum(wts[:layer + 1], axis=0, keepdims=True) - wts[0:1]

    def stack(get):
        return jnp.concatenate([get(s, h) for s in range(nseq) for h in range(H_B)], axis=0)

    lbs = stack(lambda s, h: jnp.broadcast_to(lb_row[:, h * DK_B:(h + 1) * DK_B], (chunk, DK_B)))
    tloc = lax.broadcasted_iota(jnp.int32, (rows, DK_B), 0) & (chunk - 1)
    tri = tri_ref[...]
    gain = gain_ref[...]

    for ci in range(nchunks):
        r0 = ci * chunk

        def seg(col, r0=r0):
            return stack(lambda s, h: hg_ref[col, s, r0:r0 + chunk, h * DK_B:(h + 1) * DK_B])

        zq, zf, vi, zg = seg(0), seg(1), seg(2), seg(3)
        logf = jnp.log(lbs + (1.0 - lbs) * _sigmoid(zf))
        kk = (1.0 - lbs) * _sigmoid(-zf)
        qs = zq * _sigmoid(zq)
        if valid < chunk:
            ok = tloc < valid
            logf = jnp.where(ok, logf, 0.0)
            kk = jnp.where(ok, kk, 0.0)
            qs = jnp.where(ok, qs, 0.0)

        l_hi = logf.astype(BF16)
        rem = logf - l_hi.astype(F32)
        l_mid = rem.astype(BF16)
        l_lo = (rem - l_mid.astype(F32)).astype(BF16)
        cum = _dot(tri, l_hi) + _dot(tri, l_mid) + _dot(tri, l_lo)

        att = _dot_nt(qs.astype(BF16), kk.astype(BF16)) * masks_ref[len(levels)]
        for li, n in enumerate(levels):
            ref_row = _seg_bcast(cum, n, n // 2 - 1)
            e = jnp.exp(-jnp.abs(cum - ref_row))
            upper = (tloc & (n - 1)) >= (n // 2)
            qn = jnp.where(upper, qs * e, 0.0).astype(BF16)
            kn = jnp.where(upper, 0.0, kk * e).astype(BF16)
            att = att + _dot_nt(qn, kn) * masks_ref[li]

        o = _dot(att.astype(BF16), vi.astype(BF16))
        qe = qs * jnp.exp(cum)
        o = o + jnp.concatenate(
            [_dot(qe[g * chunk:(g + 1) * chunk].astype(BF16), s_scr[g].astype(BF16)) for g in range(groups)],
            axis=0)

        on = o * lax.rsqrt(jnp.mean(o * o, axis=-1, keepdims=True) + EPS) * gain
        og = on * (zg * _sigmoid(zg))
        for s in range(nseq):
            ob_ref[s, r0:r0 + chunk, :] = jnp.concatenate(
                [og[(s * H_B + h) * chunk:(s * H_B + h + 1) * chunk] for h in range(H_B)], axis=1)

        last = _seg_bcast(cum, chunk, chunk - 1)
        kd_t = (kk * jnp.exp(last - cum)).T
        dec_t = jnp.exp(last).T
        colg = lax.broadcasted_iota(jnp.int32, (DK_B, rows), 1) >> (chunk.bit_length() - 1)
        vb = vi.astype(BF16)
        for g in range(groups):
            kd_g = jnp.where(colg == g, kd_t, 0.0).astype(BF16)
            s_scr[g] = dec_t[:, g * chunk:g * chunk + 1] * s_scr[g] + _dot(kd_g, vb)

    @pl.when(j == pl.num_programs(1) - 1)
    def _():
        sout_ref[...] = s_scr[...].reshape(nseq, H_B, DK_B, DK_B)


def _hgrn(hg4, s0, lb_logits, gain, *, layer, chunk, valid, nseq, nchunks):
    _, nb, tp, _ = hg4.shape
    rb = chunk * nchunks
    groups = nseq * H_B
    rows = groups * chunk
    masks = _hgrn_masks(chunk, groups)
    t = np.arange(rows)
    tri = jnp.asarray(((t[:, None] // chunk == t[None, :] // chunk) & (t[None, :] <= t[:, None])).astype(np.float32),
                      dtype=BF16)
    kern = functools.partial(_hgrn_kernel, layer=layer, chunk=chunk, valid=valid, nseq=nseq, nchunks=nchunks)
    return pl.pallas_call(
        kern,
        out_shape=(
            jax.ShapeDtypeStruct((nb, tp, W_B), F32),
            jax.ShapeDtypeStruct((nb, H_B, DK_B, DK_B), F32),
        ),
        grid=(nb // nseq, tp // rb),
        in_specs=[
            pl.BlockSpec(lb_logits.shape, lambda b, j: (0, 0)),
            pl.BlockSpec((1, DK_B), lambda b, j: (0, 0)),
            pl.BlockSpec(masks.shape, lambda b, j: (0, 0, 0)),
            pl.BlockSpec((rows, rows), lambda b, j: (0, 0)),
            pl.BlockSpec((4, nseq, rb, W_B), lambda b, j: (0, b, j, 0)),
            pl.BlockSpec((nseq, H_B, DK_B, DK_B), lambda b, j: (b, 0, 0, 0)),
        ],
        out_specs=(
            pl.BlockSpec((nseq, rb, W_B), lambda b, j: (b, j, 0)),
            pl.BlockSpec((nseq, H_B, DK_B, DK_B), lambda b, j: (b, 0, 0, 0)),
        ),
        scratch_shapes=[pltpu.VMEM((groups, DK_B, DK_B), F32)],
        compiler_params=_cparams(("parallel", "arbitrary")),
        name="hgrn",
    )(lb_logits, gain, masks, tri, hg4, s0)


def _mix_kernel(oa_ref, ob_ref, r_ref, x_ref, gt_ref, wpa_ref, wpb_ref, wo_ref, out_ref):
    half = r_ref.shape[2]
    pa = _dot(oa_ref[...], wpa_ref[...])
    pb = _dot(ob_ref[...].astype(BF16), wpb_ref[...])
    merged = jnp.concatenate(
        [r_ref[c].astype(F32) * pa[:, c * half:(c + 1) * half]
         + r_ref[2 + c].astype(F32) * pb[:, c * half:(c + 1) * half] for c in range(2)], axis=1)
    y = _dot(merged.astype(BF16), wo_ref[...])
    out_ref[...] = x_ref[...] + gt_ref[...] * y


def _mix(oa, ob, r, x2, gt, wpa, wpb, wo, *, tm, tiles_per_group):
    rows, d = x2.shape
    rb = gt.shape[1]
    return pl.pallas_call(
        _mix_kernel,
        out_shape=jax.ShapeDtypeStruct((rows, d), F32),
        grid=(rows // tm,),
        in_specs=[
            pl.BlockSpec((tm, W_A), lambda i: (i, 0)),
            pl.BlockSpec((tm, W_B), lambda i: (i, 0)),
            pl.BlockSpec((4, tm, IN_TN), lambda i: (0, i, 0)),
            pl.BlockSpec((tm, d), lambda i: (i, 0)),
            pl.BlockSpec((None, rb, d), lambda i: (i // tiles_per_group, 0, 0)),
            pl.BlockSpec((W_A, d), lambda i: (0, 0)),
            pl.BlockSpec((W_B, d), lambda i: (0, 0)),
            pl.BlockSpec((d, d), lambda i: (0, 0)),
        ],
        out_specs=pl.BlockSpec((tm, d), lambda i: (i, 0)),
        compiler_params=_cparams(("parallel",)),
        name="mix",
    )(oa, ob, r, x2, gt, wpa, wpb, wo)


FFN_CHUNKS = 2
FFN_SUB = 256


def _gelu_tanh(x):
    return 0.5 * x * (1.0 + jnp.tanh(0.7978845608028654 * (x + 0.044715 * x * x * x)))


def _ffn_kernel(x_ref, g_ref, sc_ref, sh_ref, gt_ref, wu_ref, wv_ref, cw_ref, cb_ref, wd_ref, e1_ref, e2_ref,
                out_ref, ut_ref, h_scr, acc_scr, carry_scr, *, seq, tiles_per_seq):
    i = pl.program_id(0)
    c = pl.program_id(1)
    tm = x_ref.shape[0]
    tc = wu_ref.shape[1]

    @pl.when(c == 0)
    def _():
        x = x_ref[...]
        ms = jnp.mean(x * x, axis=-1, keepdims=True)
        y = x * lax.rsqrt(ms + EPS) * g_ref[...]
        h_scr[...] = (y * (1.0 + sc_ref[...]) + sh_ref[...]).astype(BF16)
        acc_scr[...] = jnp.zeros_like(acc_scr)

    h = h_scr[...]
    col0 = pl.multiple_of(c * tc, tc)
    if tiles_per_seq >= 1:
        @pl.when((i % tiles_per_seq) == 0)
        def _():
            carry_scr[:, pl.ds(col0, tc)] = e1_ref[...]

    acts = []
    for a in range(0, tc, FFN_SUB):
        b = min(a + FFN_SUB, tc)
        w = b - a
        u = _dot(h, wu_ref[:, a:b])
        v = _dot(h, wv_ref[:, a:b])
        row = lax.broadcasted_iota(jnp.int32, (tm, w), 0)
        if tiles_per_seq >= 1:
            cols = pl.ds(pl.multiple_of(col0 + a, V7X_LANES), w)
            prev = carry_scr[:, cols]
            m1 = jnp.where(row >= 1, pltpu.roll(u, 1, 0), jnp.broadcast_to(prev[1:2], (tm, w)))
            m2 = jnp.where(row >= 2, pltpu.roll(u, 2, 0),
                           jnp.where(row == 1, jnp.broadcast_to(prev[1:2], (tm, w)),
                                     jnp.broadcast_to(prev[0:1], (tm, w))))
            carry_scr[:, cols] = u[tm - 2:tm, :]
        else:
            t = row & (seq - 1)
            m1 = jnp.where(t >= 1, pltpu.roll(u, 1, 0), e1_ref[:, a:b])
            m2 = jnp.where(t >= 2, pltpu.roll(u, 2, 0), e2_ref[:, a:b])
        conv = cb_ref[:, a:b] + m2 * cw_ref[0:1, a:b] + m1 * cw_ref[1:2, a:b] + u * cw_ref[2:3, a:b]
        acts.append((_gelu_tanh(conv) * v).astype(BF16))
        ut_ref[:, a:b] = u[tm - ut_ref.shape[0]:, :]
    acc_scr[...] = acc_scr[...] + _dot(jnp.concatenate(acts, axis=1), wd_ref[...])

    @pl.when(c == pl.num_programs(1) - 1)
    def _():
        out_ref[...] = x_ref[...] + gt_ref[...] * acc_scr[...]


def _ffn(x2, g, sc, sh, gt, wup, cw, cb, wd, e1, e2, *, tm, tiles_per_group, seq, ut_rows):
    rows, d = x2.shape
    d_ff = wd.shape[0]
    nc = FFN_CHUNKS
    tc = d_ff // nc
    assert tc * nc == d_ff and tc % V7X_LANES == 0
    rb = sc.shape[1]
    tiles_per_seq = seq // tm
    mod_spec = pl.BlockSpec((None, rb, d), lambda i, c: (i // tiles_per_group, 0, 0))
    if tiles_per_seq >= 1:
        e_spec = pl.BlockSpec((None, 2, tc), lambda i, c: (i // tiles_per_seq, 0, c))
    else:
        e_spec = pl.BlockSpec((tm, tc), lambda i, c: (i, c))
    kern = functools.partial(_ffn_kernel, seq=seq, tiles_per_seq=tiles_per_seq)
    return pl.pallas_call(
        kern,
        out_shape=(
            jax.ShapeDtypeStruct((rows, d), F32),
            jax.ShapeDtypeStruct((rows // tm, ut_rows, d_ff), F32),
        ),
        grid=(rows // tm, nc),
        in_specs=[
            pl.BlockSpec((tm, d), lambda i, c: (i, 0)),
            pl.BlockSpec((1, d), lambda i, c: (0, 0)),
            mod_spec, mod_spec, mod_spec,
            pl.BlockSpec((d, tc), lambda i, c: (0, c)),
            pl.BlockSpec((d, tc), lambda i, c: (0, nc + c)),
            pl.BlockSpec((CONV_W, tc), lambda i, c: (0, c)),
            pl.BlockSpec((1, tc), lambda i, c: (0, c)),
            pl.BlockSpec((tc, d), lambda i, c: (c, 0)),
            e_spec, e_spec,
        ],
        out_specs=(
            pl.BlockSpec((tm, d), lambda i, c: (i, 0)),
            pl.BlockSpec((None, ut_rows, tc), lambda i, c: (i, 0, c)),
        ),
        scratch_shapes=[
            pltpu.VMEM((tm, d), BF16),
            pltpu.VMEM((tm, d), F32),
            pltpu.VMEM((2, d_ff), F32),
        ],
        compiler_params=_cparams(("arbitrary", "arbitrary")),
        name="ffn",
    )(x2, g, sc, sh, gt, wup, wup, cw, cb, wd, e1, e2)


def _group_mods(mod, d, rows_per_seq, tile_rows):
    parts = [mod[:, k * d:(k + 1) * d] for k in range(6)]
    if rows_per_seq >= tile_rows:
        return [p[:, None, :] for p in parts]
    return [jnp.repeat(p, rows_per_seq, axis=0)[None] for p in parts]


def _layer_in(x3, mods, p, *, tm_in, consts):
    nseq, seq, d = x3.shape
    rows = nseq * seq
    tiles_per_group_in = max(seq // tm_in, 1) if seq >= tm_in else rows // tm_in
    sh1, sc1 = mods[0], mods[1]
    kv_t = seq >= tm_in
    k_gain = p['k_gain'].reshape(W_A, 1) if kv_t else p['k_gain']
    return _in_proj(x3.reshape(rows, d), p['g_norm1'], sc1, sh1, p['w_in'], p['wkv_t'], consts['bd'],
                    p['q_gain'], k_gain, tm=tm_in, tiles_per_group=tiles_per_group_in, kv_t=kv_t)


def _layer_out(x3, mods, p, layer, proj, oa, *, s0, conv_prev, tm, kv_t):
    nseq, seq, d = x3.shape
    rows = nseq * seq
    x2 = x3.reshape(rows, d)
    tiles_per_group = max(seq // tm, 1) if seq >= tm else rows // tm
    sh1, sc1, gt1, sh2, sc2, gt2 = mods
    q, k, v, hg, r = proj

    hg4 = hg.reshape(4, nseq, seq, W_B)
    if seq >= HGRN_CHUNK:
        ob3, s_new = _hgrn(hg4, s0, p['lb_logits'], p['hgrn_gain'],
                           layer=layer, chunk=HGRN_CHUNK, valid=HGRN_CHUNK, nseq=1, nchunks=4)
        ob = ob3.reshape(rows, W_B)
    else:
        padded = 8
        hg4 = jnp.pad(hg4, ((0, 0), (0, 0), (0, padded - seq), (0, 0)))
        ob3, s_new = _hgrn(hg4, s0, p['lb_logits'], p['hgrn_gain'],
                           layer=layer, chunk=padded, valid=seq, nseq=4, nchunks=1)
        ob = ob3[:, :seq].reshape(rows, W_B)

    x1 = _mix(oa, ob, r, x2, gt1, p['w_pa'], p['w_pb'], p['w_o'], tm=tm, tiles_per_group=tiles_per_group)

    d_ff = p['w_down'].shape[0]
    if seq >= tm:
        e1 = e2 = conv_prev
        ut_rows = 8
    else:
        z = jnp.zeros((nseq, seq - 2, d_ff), F32)
        e1 = jnp.concatenate([conv_prev[:, 1:2], jnp.zeros((nseq, seq - 1, d_ff), F32)], axis=1).reshape(rows, d_ff)
        e2 = jnp.concatenate([conv_prev, z], axis=1).reshape(rows, d_ff)
        ut_rows = tm
    y, ut = _ffn(x1, p['g_norm2'], sc2, sh2, gt2, p['w_up'], p['conv_w'], p['conv_b'], p['w_down'], e1, e2,
                 tm=tm, tiles_per_group=tiles_per_group, seq=seq, ut_rows=ut_rows)
    if seq >= tm:
        tps = seq // tm
        conv_new = ut.reshape(nseq, tps, 8, d_ff)[:, tps - 1, 6:8]
    else:
        conv_new = ut.reshape(nseq, seq, d_ff)[:, seq - 2:]
    if kv_t:
        k_out = k.reshape(nseq, H_A, DH_A, seq).transpose(0, 3, 1, 2)
        v_out = v.reshape(nseq, H_A, DH_A, seq).transpose(0, 3, 1, 2)
    else:
        k_out = k.reshape(nseq, seq, H_A, DH_A)
        v_out = v.reshape(nseq, seq, H_A, DH_A)
    return y.reshape(nseq, seq, d), k_out, v_out, s_new, conv_new


def kernel(x_prompt, x_sample, cache_k, cache_v, state_hgrn, state_conv, page_table, c_prompt, c_sample,
           w_ada, b_ada, g_norm1, w_in, q_gain, k_gain, sb_bias, hgrn_lb_logits, hgrn_gain, w_pa, w_pb, w_o,
           g_norm2, w_up, conv_w, conv_b, w_down):
    depth = w_ada.shape[0]
    n_pr, seq, d = x_prompt.shape
    n_dec, dec_seq, _ = x_sample.shape
    d_ff = w_down.shape[1]
    n_pool, page = cache_k.shape[1], cache_k.shape[2]

    mod_all = _ada(jnp.concatenate([c_prompt, c_sample], axis=0), w_ada, b_ada)

    head = np.arange(W_A) // DH_A
    consts = {'bd': jnp.asarray((head[:, None] == head[None, :]).astype(np.float32) / DH_A, dtype=BF16)}
    uo_prompt = _suffix_matrix(ATT_TK)
    uo_page = _suffix_matrix(page)
    ck = jnp.transpose(cache_k, (0, 1, 3, 4, 2)).reshape(depth, n_pool, W_A, page)
    cv = jnp.transpose(cache_v, (0, 1, 3, 4, 2)).reshape(depth, n_pool, W_A, page)
    rowh = np.arange(dec_seq * H_A) % H_A
    qmask = jnp.asarray((rowh[:, None] == head[None, :]), dtype=BF16)

    yp, ys = x_prompt, x_sample
    outs = [[] for _ in range(8)]
    for l in range(depth):
        p = {
            'g_norm1': g_norm1[l][None], 'g_norm2': g_norm2[l][None],
            'w_in': w_in[l].astype(BF16).reshape(d, -1, IN_TN).transpose(1, 0, 2),
            'w_pa': w_pa[l].astype(BF16), 'w_pb': w_pb[l].astype(BF16),
            'wkv_t': w_in[l][:, W_A:3 * W_A].T.reshape(2, W_A, d).astype(BF16),
            'w_o': w_o[l].astype(BF16), 'w_up': w_up[l].astype(BF16), 'w_down': w_down[l].astype(BF16),
            'q_gain': jnp.tile(q_gain[l], H_A)[None], 'k_gain': jnp.tile(k_gain[l], H_A)[None],
            'lb_logits': hgrn_lb_logits, 'hgrn_gain': hgrn_gain[l][None],
            'conv_w': conv_w[l], 'conv_b': conv_b[l][None],
        }

        tm_p, tm_in_p, tm_s = 512, min(1024, seq), n_dec * dec_seq
        mods_p = _group_mods(mod_all[l, :n_pr], d, seq, tm_p)
        mods_s = _group_mods(mod_all[l, n_pr:], d, dec_seq, tm_s)
        proj_p = _layer_in(yp, mods_p, p, tm_in=tm_in_p, consts=consts)
        proj_s = _layer_in(ys, mods_s, p, tm_in=tm_s, consts=consts)

        qs = proj_s[0].reshape(n_dec, dec_seq, 1, W_A)
        qbd = (jnp.broadcast_to(qs, (n_dec, dec_seq, H_A, W_A)).reshape(n_dec, dec_seq * H_A, W_A) * qmask[None])
        bias_rows = jnp.broadcast_to(jnp.tile(sb_bias[l], dec_seq)[:, None], (dec_seq * H_A, V7X_LANES))
        knew = jnp.pad(proj_s[1].reshape(n_dec, dec_seq, W_A), ((0, 0), (0, 8 - dec_seq), (0, 0)))
        vnew = jnp.pad(proj_s[2].reshape(n_dec, dec_seq, W_A), ((0, 0), (0, 8 - dec_seq), (0, 0)))
        oa_p, oa_s = _attention(proj_p[0], proj_p[1], proj_p[2], sb_bias[l], uo_prompt,
                                page_table, qbd, bias_rows, knew, vnew, uo_page, ck, cv,
                                batch=n_pr, seq=seq, layer=l)
        oa_s = oa_s.reshape(n_dec * dec_seq, W_A).astype(BF16)

        yp, kp, vp, hp, cp = _layer_out(
            yp, mods_p, p, l, proj_p, oa_p,
            s0=jnp.zeros((n_pr, H_B, DK_B, DK_B), F32), conv_prev=jnp.zeros((n_pr, CONV_W - 1, d_ff), F32),
            tm=tm_p, kv_t=True)
        ys, ksm, vsm, hs, cs = _layer_out(
            ys, mods_s, p, l, proj_s, oa_s, s0=state_hgrn[l], conv_prev=state_conv[l], tm=tm_s, kv_t=False)
        for lst, val in zip(outs, (kp, vp, ksm, vsm, hp, hs, cp, cs)):
            lst.append(val)

    return (yp, ys) + tuple(jnp.stack(o) for o in outs)
```

```python
import functools

import numpy as np
import jax
import jax.numpy as jnp
from jax import lax
from jax.experimental import pallas as pl
from jax.experimental.pallas import tpu as pltpu

F32 = jnp.float32
BF16 = jnp.bfloat16

EPS = 1e-6
V7X_LANES = 128
V7X_VMEM_LIMIT_BYTES = 56 * 1024 * 1024

H_A = 8
DH_A = 64
W_A = H_A * DH_A
H_B = 4
DK_B = 128
W_B = H_B * DK_B
CONV_W = 3
HGRN_CHUNK = 64


def _cparams(sem):
    return pltpu.CompilerParams(dimension_semantics=sem, vmem_limit_bytes=V7X_VMEM_LIMIT_BYTES)


def _sigmoid(x):
    return 1.0 / (1.0 + jnp.exp(-x))


def _dot(a, b):
    return jnp.dot(a, b, preferred_element_type=F32)


def _dot_nt(a, b):
    return lax.dot_general(a, b, (((1,), (1,)), ((), ())), preferred_element_type=F32)


def _split2(x):
    hi = x.astype(BF16)
    lo = (x - hi.astype(F32)).astype(BF16)
    return hi, lo


def _ada_kernel(c_ref, w_ref, b_ref, o_ref):
    c = c_ref[...]
    s = c * _sigmoid(c)
    s_hi, s_lo = _split2(s)
    w_hi, w_lo = _split2(w_ref[...])
    acc = _dot(s_hi, w_hi) + _dot(s_hi, w_lo) + _dot(s_lo, w_hi)
    o_ref[...] = acc + b_ref[...]


def _ada(c_all, w_ada, b_ada):
    depth, d, n6 = w_ada.shape
    nc = c_all.shape[0]
    tn = 1536
    return pl.pallas_call(
        _ada_kernel,
        out_shape=jax.ShapeDtypeStruct((depth, nc, n6), F32),
        grid=(depth, n6 // tn),
        in_specs=[
            pl.BlockSpec((nc, d), lambda l, j: (0, 0)),
            pl.BlockSpec((None, d, tn), lambda l, j: (l, 0, j)),
            pl.BlockSpec((None, 1, tn), lambda l, j: (l, 0, j)),
        ],
        out_specs=pl.BlockSpec((None, nc, tn), lambda l, j: (l, 0, j)),
        compiler_params=_cparams(("parallel", "parallel")),
        name="ada_mod",
    )(c_all, w_ada, b_ada.reshape(depth, 1, n6))


IN_TN = 512


def _in_kernel(x_ref, g_ref, sc_ref, sh_ref, w_ref, wkv_ref, bd_ref, qg_ref, kg_ref,
               q_ref, k_ref, v_ref, hg_ref, r_ref, h_scr, *, kv_t):
    j = pl.program_id(1)

    @pl.when(j == 0)
    def _():
        x = x_ref[...]
        ms = jnp.mean(x * x, axis=-1, keepdims=True)
        y = x * lax.rsqrt(ms + EPS) * g_ref[...]
        h_scr[...] = (y * (1.0 + sc_ref[...]) + sh_ref[...]).astype(BF16)

    def proj():
        return _dot(h_scr[...], w_ref[...])

    def proj_t():
        return _dot_nt(wkv_ref[...], h_scr[...])

    def head_norm(t):
        ms = _dot((t * t).astype(BF16), bd_ref[...])
        return t * lax.rsqrt(ms + EPS)

    @pl.when(j == 0)
    def _():
        q_ref[...] = (head_norm(proj()) * qg_ref[...] * (DH_A ** -0.5)).astype(BF16)

    @pl.when(j == 1)
    def _():
        if kv_t:
            t = proj_t()
            t3 = t.reshape(H_A, DH_A, t.shape[1])
            ms = jnp.mean(t3 * t3, axis=1, keepdims=True)
            k_ref[...] = (t3 * lax.rsqrt(ms + EPS)).reshape(t.shape) * kg_ref[...]
        else:
            t = proj()
            sq_hi, sq_lo = _split2(t * t)
            ms = _dot(sq_hi, bd_ref[...]) + _dot(sq_lo, bd_ref[...])
            k_ref[...] = t * lax.rsqrt(ms + EPS) * kg_ref[...]

    @pl.when(j == 2)
    def _():
        v_ref[...] = proj_t() if kv_t else proj()

    @pl.when(jnp.logical_and(j >= 3, j <= 6))
    def _():
        hg_ref[...] = proj()

    @pl.when(j >= 7)
    def _():
        r_ref[...] = _sigmoid(proj()).astype(r_ref.dtype)


def _in_proj(x2, g, sc, sh, w_tiles, wkv_t, bd, qg, kg, *, tm, tiles_per_group, kv_t):
    rows, d = x2.shape
    nj = w_tiles.shape[0]
    rb = sc.shape[1]
    mod_spec = pl.BlockSpec((None, rb, d), lambda i, j: (i // tiles_per_group, 0, 0))
    kv_sel = lambda j: jnp.clip(j - 1, 0, 1)
    if kv_t:
        nseq = rows // (tiles_per_group * tm)
        kv_shape = (nseq, W_A, tiles_per_group * tm)
        kv_spec = pl.BlockSpec((None, W_A, tm), lambda i, j: (i // tiles_per_group, 0, i % tiles_per_group))
        w_spec = pl.BlockSpec((None, d, IN_TN), lambda i, j: (jnp.where((j == 1) | (j == 2), 0, j), 0, 0))
        kg_spec = pl.BlockSpec((W_A, 1), lambda i, j: (0, 0))
    else:
        kv_shape = (rows, W_A)
        kv_spec = pl.BlockSpec((tm, IN_TN), lambda i, j: (i, 0))
        w_spec = pl.BlockSpec((None, d, IN_TN), lambda i, j: (j, 0, 0))
        kg_spec = pl.BlockSpec((1, W_A), lambda i, j: (0, 0))
    return pl.pallas_call(
        functools.partial(_in_kernel, kv_t=kv_t),
        out_shape=(
            jax.ShapeDtypeStruct((rows, W_A), BF16),
            jax.ShapeDtypeStruct(kv_shape, F32),
            jax.ShapeDtypeStruct(kv_shape, F32),
            jax.ShapeDtypeStruct((4, rows, W_B), F32),
            jax.ShapeDtypeStruct((4, rows, IN_TN), BF16),
        ),
        grid=(rows // tm, nj),
        in_specs=[
            pl.BlockSpec((tm, d), lambda i, j: (i, 0)),
            pl.BlockSpec((1, d), lambda i, j: (0, 0)),
            mod_spec,
            mod_spec,
            w_spec,
            pl.BlockSpec((None, W_A, d), lambda i, j: (kv_sel(j), 0, 0)),
            pl.BlockSpec((W_A, W_A), lambda i, j: (0, 0)),
            pl.BlockSpec((1, W_A), lambda i, j: (0, 0)),
            kg_spec,
        ],
        out_specs=(
            pl.BlockSpec((tm, IN_TN), lambda i, j: (i, 0)),
            kv_spec,
            kv_spec,
            pl.BlockSpec((None, tm, IN_TN), lambda i, j: (jnp.clip(j - 3, 0, 3), i, 0)),
            pl.BlockSpec((None, tm, IN_TN), lambda i, j: (jnp.clip(j - 7, 0, 3), i, 0)),
        ),
        scratch_shapes=[pltpu.VMEM((tm, d), BF16)],
        compiler_params=_cparams(("parallel", "arbitrary")),
        name="in_proj",
    )(x2, g, sc, sh, w_tiles, wkv_t, bd, qg, kg)


def _sb_keep(z, mask):
    sp = jnp.maximum(z, 0.0) + jnp.log(1.0 + jnp.exp(-jnp.abs(z)))
    spm = sp if mask is None else jnp.where(mask, sp, 0.0)
    return spm, z - sp


def _sb_weights(log_beta, after, c, mask):
    n = log_beta.shape[1]
    c_full = c if n == V7X_LANES else jnp.concatenate([c] * (n // V7X_LANES), axis=1)
    a = jnp.exp(log_beta + after + c_full)
    return a if mask is None else jnp.where(mask, a, 0.0)


def _sb_multi(problems):
    keeps = [[_sb_keep(z, m) for z, m in zip(zs, masks)] for zs, _, _, masks in problems]
    afters = [[_dot(sp.astype(BF16), neg_u) for sp, _ in kp] for kp, (_, _, neg_u, _) in zip(keeps, problems)]
    tots = [[jnp.sum(sp, axis=1, keepdims=True) for sp, _ in kp] for kp in keeps]
    out = []
    for kp, af, tt, (_, c, _, masks) in zip(keeps, afters, tots, problems):
        ws = []
        for (_, log_beta), after, tot, m in zip(kp, af, tt, masks):
            ws.append(_sb_weights(log_beta, after, c, m).astype(BF16))
            c = c - tot
        out.append((ws, c))
    return out


def _sb_blocks(zs, c, neg_u, masks):
    return _sb_multi([(zs, c, neg_u, masks)])[0]


def _suffix_matrix(n):
    j = np.arange(n)[:, None]
    s = np.arange(n)[None, :]
    return jnp.asarray(-(j > s).astype(np.float32), dtype=BF16)


ATT_TQ = 256
ATT_TK = 256


def _attn_kernel(pt_ref, bias_ref, q_ref, kt_ref, vt_ref, u_ref, qbd_ref, sbias_ref, knew_ref, vnew_ref, up_ref,
                 *rest, npg):
    k_refs = rest[:npg]
    v_refs = rest[npg:2 * npg]
    o_ref, os_ref, kv_scr, acc_scr, c_scr, sacc_scr, sc_scr = rest[2 * npg:]
    hp = pl.program_id(1)
    qi = pl.program_id(2)
    tq, tk = ATT_TQ, ATT_TK
    nkb = kv_scr.shape[1]
    first_head = lax.broadcasted_iota(jnp.int32, (1, V7X_LANES), 1) < DH_A
    nrow = qbd_ref.shape[0]
    dec_seq = nrow // H_A
    page = up_ref.shape[0]
    qbd = qbd_ref[...]
    sbias = sbias_ref[...]
    up = up_ref[...]

    @pl.when(qi == 0)
    def _():
        for kb in range(nkb):
            kv_scr[0, kb] = kt_ref[:, kb * tk:(kb + 1) * tk].astype(BF16)
            kv_scr[1, kb] = vt_ref[:, kb * tk:(kb + 1) * tk].astype(BF16)
        pad = jnp.zeros((page - knew_ref.shape[0], W_A), F32)
        kn = jnp.concatenate([knew_ref[...], pad], axis=0).astype(BF16)
        vn = jnp.concatenate([vnew_ref[...], pad], axis=0).astype(BF16)
        t = lax.broadcasted_iota(jnp.int32, (nrow, page), 0) >> 3
        s = lax.broadcasted_iota(jnp.int32, (nrow, page), 1)
        ws, c = _sb_blocks([_dot_nt(qbd, kn) + sbias], jnp.zeros((nrow, V7X_LANES), F32), up, [s < t])
        sacc_scr[...] = _dot(ws[0], vn)
        sc_scr[...] = c

    q = q_ref[...]
    zero = jnp.zeros_like(q)
    q2 = jnp.concatenate([jnp.where(first_head, q, zero), jnp.where(first_head, zero, q)], axis=0)
    u = u_ref[...]
    b0 = bias_ref[2 * hp]
    b1 = bias_ref[2 * hp + 1]
    row = lax.broadcasted_iota(jnp.int32, (2 * tq, tk), 0) & (tq - 1)
    col = lax.broadcasted_iota(jnp.int32, (2 * tq, tk), 1)

    def logits(kb):
        s = _dot(q2, kv_scr[0, kb])
        return jnp.concatenate([s[:tq] + b0, s[tq:] + b1], axis=0)

    def steps(kbs, masks):
        ws, c = _sb_blocks([logits(kb) for kb in kbs], c_scr[...], u, masks)
        acc = acc_scr[...]
        for kb, w in zip(kbs, ws):
            acc = acc + _dot_nt(w, kv_scr[1, kb])
        acc_scr[...] = acc
        c_scr[...] = c

    zs = [_dot(qbd, k_refs[i][...].astype(BF16)) + sbias for i in range(npg)]
    ws, c = _sb_blocks(zs, sc_scr[...], up, [None] * npg)
    sacc = sacc_scr[...]
    for i in range(npg):
        sacc = sacc + _dot_nt(ws[i], v_refs[i][...].astype(BF16))
    sacc_scr[...] = sacc
    sc_scr[...] = c

    acc_scr[...] = jnp.zeros_like(acc_scr)
    c_scr[...] = jnp.zeros_like(c_scr)
    steps([qi], [col < row])

    def body(i, carry):
        kb = qi - 1 - 4 * i
        steps([kb, kb - 1, kb - 2, kb - 3], [None] * 4)
        return carry

    lax.fori_loop(0, qi // 4, body, 0)
    rem = qi & 3

    @pl.when((rem & 2) != 0)
    def _():
        steps([rem - 1, rem - 2], [None] * 2)

    @pl.when((rem & 1) != 0)
    def _():
        steps([0], [None])

    o_ref[...] = jnp.where(first_head, acc_scr[:tq], acc_scr[tq:]).astype(o_ref.dtype)

    @pl.when(qi == pl.num_programs(2) - 1)
    def _():
        r = lax.broadcasted_iota(jnp.int32, (nrow, W_A), 0)
        l = lax.broadcasted_iota(jnp.int32, (nrow, W_A), 1)
        own = (r & (H_A - 1)) == (l >> 6)
        om = jnp.where(own, sacc_scr[...], 0.0)
        os_ref[...] = jnp.sum(om.reshape(dec_seq, H_A, W_A), axis=1)


def _attention(q, kt, vt, sb_bias, u, page_table, qbd, bias_rows, knew, vnew, u_page, cache_k, cache_v,
               *, batch, seq, layer):
    rows = q.shape[0]
    nq = seq // ATT_TQ
    npair = H_A // 2
    nb, nrow, _ = qbd.shape
    n_pages = page_table.shape[1]
    page = cache_k.shape[3]
    dec_seq = nrow // H_A
    assert nb == batch * npair and n_pages % nq == 0
    npg = n_pages // nq

    def page_spec(i):
        def imap(b, hp, qi, pt):
            return (layer, pt[b * npair + hp, n_pages - 1 - (qi * npg + i)], 0, 0)
        return pl.BlockSpec((None, None, W_A, page), imap)

    sample_seq = lambda b, hp, qi, pt: (b * npair + hp, 0, 0)
    grid_spec = pltpu.PrefetchScalarGridSpec(
        num_scalar_prefetch=1,
        grid=(batch, npair, nq),
        in_specs=[
            pl.BlockSpec(memory_space=pltpu.SMEM),
            pl.BlockSpec((ATT_TQ, V7X_LANES), lambda b, hp, qi, pt: (b * nq + qi, hp)),
            pl.BlockSpec((None, V7X_LANES, seq), lambda b, hp, qi, pt: (b, hp, 0)),
            pl.BlockSpec((None, V7X_LANES, seq), lambda b, hp, qi, pt: (b, hp, 0)),
            pl.BlockSpec((ATT_TK, ATT_TK), lambda b, hp, qi, pt: (0, 0)),
            pl.BlockSpec((None, nrow, W_A), sample_seq),
            pl.BlockSpec((nrow, V7X_LANES), lambda b, hp, qi, pt: (0, 0)),
            pl.BlockSpec((None, 8, W_A), sample_seq),
            pl.BlockSpec((None, 8, W_A), sample_seq),
            pl.BlockSpec((page, page), lambda b, hp, qi, pt: (0, 0)),
        ] + [page_spec(i) for i in range(npg)] + [page_spec(i) for i in range(npg)],
        out_specs=(
            pl.BlockSpec((ATT_TQ, V7X_LANES), lambda b, hp, qi, pt: (b * nq + qi, hp)),
            pl.BlockSpec((None, dec_seq, W_A), sample_seq),
        ),
        scratch_shapes=[
            pltpu.VMEM((2, seq // ATT_TK, V7X_LANES, ATT_TK), BF16),
            pltpu.VMEM((2 * ATT_TQ, V7X_LANES), F32),
            pltpu.VMEM((2 * ATT_TQ, V7X_LANES), F32),
            pltpu.VMEM((nrow, W_A), F32),
            pltpu.VMEM((nrow, V7X_LANES), F32),
        ],
    )
    return pl.pallas_call(
        functools.partial(_attn_kernel, npg=npg),
        out_shape=(
            jax.ShapeDtypeStruct((rows, W_A), BF16),
            jax.ShapeDtypeStruct((nb, dec_seq, W_A), F32),
        ),
        grid_spec=grid_spec,
        compiler_params=_cparams(("parallel", "parallel", "arbitrary")),
        name="attention",
    )(page_table, sb_bias, q, kt, vt, u, qbd, bias_rows, knew, vnew, u_page,
      *([cache_k] * npg), *([cache_v] * npg))


def _seg_bcast(x, n, off):
    rows, lanes = x.shape
    if n >= 8:
        parts = [jnp.broadcast_to(x[b * n + off:b * n + off + 1, :], (n, lanes)) for b in range(rows // n)]
        return jnp.concatenate(parts, axis=0)
    sub = lax.broadcasted_iota(jnp.int32, (8, lanes), 0)
    parts = []
    for g in range(rows // 8):
        acc = jnp.broadcast_to(x[g * 8 + off:g * 8 + off + 1, :], (8, lanes))
        for sb in range(1, 8 // n):
            r = g * 8 + sb * n + off
            acc = jnp.where(sub >= sb * n, jnp.broadcast_to(x[r:r + 1, :], (8, lanes)), acc)
        parts.append(acc)
    return jnp.concatenate(parts, axis=0)


def _hgrn_levels(chunk):
    out = []
    n = chunk
    while n >= 2:
        out.append(n)
        n //= 2
    return out


def _hgrn_masks(chunk, groups):
    rows = chunk * groups
    t = np.arange(rows)
    m = [(t[:, None] // n == t[None, :] // n) & (t[:, None] % n >= n // 2) & (t[None, :] % n < n // 2)
         for n in _hgrn_levels(chunk)]
    m.append(t[:, None] == t[None, :])
    return jnp.asarray(np.stack(m).astype(np.float32))


def _hgrn_kernel(lbl_ref, gain_ref, masks_ref, tri_ref, hg_ref, s0_ref, ob_ref, sout_ref, s_scr,
                 *, layer, chunk, valid, nseq, nchunks):
    j = pl.program_id(1)
    groups = nseq * H_B
    rows = groups * chunk
    levels = _hgrn_levels(chunk)

    @pl.when(j == 0)
    def _():
        s_scr[...] = s0_ref[...].reshape(groups, DK_B, DK_B)

    lg = lbl_ref[...]
    ex = jnp.exp(lg - jnp.max(lg, axis=0, keepdims=True))
    wts = ex / jnp.sum(ex, axis=0, keepdims=True)
    lb_row = jnp.sum(wts[:layer + 1], axis=0, keepdims=True) - wts[0:1]

    def stack(get):
        return jnp.concatenate([get(s, h) for s in range(nseq) for h in range(H_B)], axis=0)

    lbs = stack(lambda s, h: jnp.broadcast_to(lb_row[:, h * DK_B:(h + 1) * DK_B], (chunk, DK_B)))
    tloc = lax.broadcasted_iota(jnp.int32, (rows, DK_B), 0) & (chunk - 1)
    tri = tri_ref[...]
    gain = gain_ref[...]

    for ci in range(nchunks):
        r0 = ci * chunk

        def seg(col, r0=r0):
            return stack(lambda s, h: hg_ref[col, s, r0:r0 + chunk, h * DK_B:(h + 1) * DK_B])

        zq, zf, vi, zg = seg(0), seg(1), seg(2), seg(3)
        logf = jnp.log(lbs + (1.0 - lbs) * _sigmoid(zf))
        kk = (1.0 - lbs) * _sigmoid(-zf)
        qs = zq * _sigmoid(zq)
        if valid < chunk:
            ok = tloc < valid
            logf = jnp.where(ok, logf, 0.0)
            kk = jnp.where(ok, kk, 0.0)
            qs = jnp.where(ok, qs, 0.0)

        l_hi = logf.astype(BF16)
        rem = logf - l_hi.astype(F32)
        l_mid = rem.astype(BF16)
        l_lo = (rem - l_mid.astype(F32)).astype(BF16)
        cum = _dot(tri, l_hi) + _dot(tri, l_mid) + _dot(tri, l_lo)

        att = _dot_nt(qs.astype(BF16), kk.astype(BF16)) * masks_ref[len(levels)]
        for li, n in enumerate(levels):
            ref_row = _seg_bcast(cum, n, n // 2 - 1)
            e = jnp.exp(-jnp.abs(cum - ref_row))
            att = att + _dot_nt((qs * e).astype(BF16), (kk * e).astype(BF16)) * masks_ref[li]

        o = _dot(att.astype(BF16), vi.astype(BF16))
        qe = qs * jnp.exp(cum)
        o = o + jnp.concatenate(
            [_dot(qe[g * chunk:(g + 1) * chunk].astype(BF16), s_scr[g].astype(BF16)) for g in range(groups)],
            axis=0)

        on = o * lax.rsqrt(jnp.mean(o * o, axis=-1, keepdims=True) + EPS) * gain
        og = on * (zg * _sigmoid(zg))
        for s in range(nseq):
            ob_ref[s, r0:r0 + chunk, :] = jnp.concatenate(
                [og[(s * H_B + h) * chunk:(s * H_B + h + 1) * chunk] for h in range(H_B)], axis=1)

        last = _seg_bcast(cum, chunk, chunk - 1)
        kd_t = (kk * jnp.exp(last - cum)).T
        dec_t = jnp.exp(last).T
        colg = lax.broadcasted_iota(jnp.int32, (DK_B, rows), 1) >> (chunk.bit_length() - 1)
        vb = vi.astype(BF16)
        for g in range(groups):
            kd_g = jnp.where(colg == g, kd_t, 0.0).astype(BF16)
            s_scr[g] = dec_t[:, g * chunk:g * chunk + 1] * s_scr[g] + _dot(kd_g, vb)

    @pl.when(j == pl.num_programs(1) - 1)
    def _():
        sout_ref[...] = s_scr[...].reshape(nseq, H_B, DK_B, DK_B)


def _hgrn(hg4, s0, lb_logits, gain, *, layer, chunk, valid, nseq, nchunks):
    _, nb, tp, _ = hg4.shape
    rb = chunk * nchunks
    groups = nseq * H_B
    rows = groups * chunk
    masks = _hgrn_masks(chunk, groups)
    t = np.arange(rows)
    tri = jnp.asarray(((t[:, None] // chunk == t[None, :] // chunk) & (t[None, :] <= t[:, None])).astype(np.float32),
                      dtype=BF16)
    kern = functools.partial(_hgrn_kernel, layer=layer, chunk=chunk, valid=valid, nseq=nseq, nchunks=nchunks)
    return pl.pallas_call(
        kern,
        out_shape=(
            jax.ShapeDtypeStruct((nb, tp, W_B), F32),
            jax.ShapeDtypeStruct((nb, H_B, DK_B, DK_B), F32),
        ),
        grid=(nb // nseq, tp // rb),
        in_specs=[
            pl.BlockSpec(lb_logits.shape, lambda b, j: (0, 0)),
            pl.BlockSpec((1, DK_B), lambda b, j: (0, 0)),
            pl.BlockSpec(masks.shape, lambda b, j: (0, 0, 0)),
            pl.BlockSpec((rows, rows), lambda b, j: (0, 0)),
            pl.BlockSpec((4, nseq, rb, W_B), lambda b, j: (0, b, j, 0)),
            pl.BlockSpec((nseq, H_B, DK_B, DK_B), lambda b, j: (b, 0, 0, 0)),
        ],
        out_specs=(
            pl.BlockSpec((nseq, rb, W_B), lambda b, j: (b, j, 0)),
            pl.BlockSpec((nseq, H_B, DK_B, DK_B), lambda b, j: (b, 0, 0, 0)),
        ),
        scratch_shapes=[pltpu.VMEM((groups, DK_B, DK_B), F32)],
        compiler_params=_cparams(("parallel", "arbitrary")),
        name="hgrn",
    )(lb_logits, gain, masks, tri, hg4, s0)


def _mix(oa_ref, ob_ref, r_ref, x_ref, gt_ref, wpa_ref, wpb_ref, wo_ref):
    half = r_ref.shape[2]
    pa = _dot(oa_ref[...], wpa_ref[...])
    pb = _dot(ob_ref[...].astype(BF16), wpb_ref[...])
    merged = jnp.concatenate(
        [r_ref[c].astype(F32) * pa[:, c * half:(c + 1) * half]
         + r_ref[2 + c].astype(F32) * pb[:, c * half:(c + 1) * half] for c in range(2)], axis=1)
    y = _dot(merged.astype(BF16), wo_ref[...])
    return x_ref[...] + gt_ref[...] * y


FFN_CHUNKS = 2
FFN_SUB = 256


def _gelu_tanh(x):
    return 0.5 * x * (1.0 + jnp.tanh(0.7978845608028654 * (x + 0.044715 * x * x * x)))


def _ffn_kernel(oa_ref, ob_ref, r_ref, x_ref, gt1_ref, wpa_ref, wpb_ref, wo_ref,
                g_ref, sc_ref, sh_ref, gt_ref, wu_ref, wv_ref, cw_ref, cb_ref, wd_ref, e1_ref, e2_ref,
                out_ref, ut_ref, x1_scr, h_scr, acc_scr, carry_scr, *, seq, tiles_per_seq):
    i = pl.program_id(0)
    c = pl.program_id(1)
    tm = x_ref.shape[0]
    tc = wu_ref.shape[1]

    @pl.when(c == 0)
    def _():
        x = _mix(oa_ref, ob_ref, r_ref, x_ref, gt1_ref, wpa_ref, wpb_ref, wo_ref)
        x1_scr[...] = x
        ms = jnp.mean(x * x, axis=-1, keepdims=True)
        y = x * lax.rsqrt(ms + EPS) * g_ref[...]
        h_scr[...] = (y * (1.0 + sc_ref[...]) + sh_ref[...]).astype(BF16)
        acc_scr[...] = jnp.zeros_like(acc_scr)

    h = h_scr[...]
    col0 = pl.multiple_of(c * tc, tc)
    if tiles_per_seq >= 1:
        @pl.when((i % tiles_per_seq) == 0)
        def _():
            carry_scr[:, pl.ds(col0, tc)] = e1_ref[...]

    acts = []
    for a in range(0, tc, FFN_SUB):
        b = min(a + FFN_SUB, tc)
        w = b - a
        u = _dot(h, wu_ref[:, a:b])
        v = _dot(h, wv_ref[:, a:b])
        row = lax.broadcasted_iota(jnp.int32, (tm, w), 0)
        if tiles_per_seq >= 1:
            cols = pl.ds(pl.multiple_of(col0 + a, V7X_LANES), w)
            prev = carry_scr[:, cols]
            m1 = jnp.where(row >= 1, pltpu.roll(u, 1, 0), jnp.broadcast_to(prev[1:2], (tm, w)))
            m2 = jnp.where(row >= 2, pltpu.roll(u, 2, 0),
                           jnp.where(row == 1, jnp.broadcast_to(prev[1:2], (tm, w)),
                                     jnp.broadcast_to(prev[0:1], (tm, w))))
            carry_scr[:, cols] = u[tm - 2:tm, :]
        else:
            t = row & (seq - 1)
            m1 = jnp.where(t >= 1, pltpu.roll(u, 1, 0), e1_ref[:, a:b])
            m2 = jnp.where(t >= 2, pltpu.roll(u, 2, 0), e2_ref[:, a:b])
        conv = cb_ref[:, a:b] + m2 * cw_ref[0:1, a:b] + m1 * cw_ref[1:2, a:b] + u * cw_ref[2:3, a:b]
        acts.append((_gelu_tanh(conv) * v).astype(BF16))
        ut_ref[:, a:b] = u[tm - ut_ref.shape[0]:, :]
    acc_scr[...] = acc_scr[...] + _dot(jnp.concatenate(acts, axis=1), wd_ref[...])

    @pl.when(c == pl.num_programs(1) - 1)
    def _():
        out_ref[...] = x1_scr[...] + gt_ref[...] * acc_scr[...]


def _mix_ffn(oa, ob, r, x2, gt1, wpa, wpb, wo, g, sc, sh, gt, wup, cw, cb, wd, e1, e2,
             *, tm, tiles_per_group, seq, ut_rows):
    rows, d = x2.shape
    d_ff = wd.shape[0]
    nc = FFN_CHUNKS
    tc = d_ff // nc
    assert tc * nc == d_ff and tc % V7X_LANES == 0
    rb = sc.shape[1]
    tiles_per_seq = seq // tm
    mod_spec = pl.BlockSpec((None, rb, d), lambda i, c: (i // tiles_per_group, 0, 0))
    if tiles_per_seq >= 1:
        e_spec = pl.BlockSpec((None, 2, tc), lambda i, c: (i // tiles_per_seq, 0, c))
    else:
        e_spec = pl.BlockSpec((tm, tc), lambda i, c: (i, c))
    kern = functools.partial(_ffn_kernel, seq=seq, tiles_per_seq=tiles_per_seq)
    return pl.pallas_call(
        kern,
        out_shape=(
            jax.ShapeDtypeStruct((rows, d), F32),
            jax.ShapeDtypeStruct((rows // tm, ut_rows, d_ff), F32),
        ),
        grid=(rows // tm, nc),
        in_specs=[
            pl.BlockSpec((tm, W_A), lambda i, c: (i, 0)),
            pl.BlockSpec((tm, W_B), lambda i, c: (i, 0)),
            pl.BlockSpec((4, tm, IN_TN), lambda i, c: (0, i, 0)),
            pl.BlockSpec((tm, d), lambda i, c: (i, 0)),
            mod_spec,
            pl.BlockSpec((W_A, d), lambda i, c: (0, 0)),
            pl.BlockSpec((W_B, d), lambda i, c: (0, 0)),
            pl.BlockSpec((d, d), lambda i, c: (0, 0)),
            pl.BlockSpec((1, d), lambda i, c: (0, 0)),
            mod_spec, mod_spec, mod_spec,
            pl.BlockSpec((d, tc), lambda i, c: (0, c)),
            pl.BlockSpec((d, tc), lambda i, c: (0, nc + c)),
            pl.BlockSpec((CONV_W, tc), lambda i, c: (0, c)),
            pl.BlockSpec((1, tc), lambda i, c: (0, c)),
            pl.BlockSpec((tc, d), lambda i, c: (c, 0)),
            e_spec, e_spec,
        ],
        out_specs=(
            pl.BlockSpec((tm, d), lambda i, c: (i, 0)),
            pl.BlockSpec((None, ut_rows, tc), lambda i, c: (i, 0, c)),
        ),
        scratch_shapes=[
            pltpu.VMEM((tm, d), F32),
            pltpu.VMEM((tm, d), BF16),
            pltpu.VMEM((tm, d), F32),
            pltpu.VMEM((2, d_ff), F32),
        ],
        compiler_params=_cparams(("arbitrary", "arbitrary")),
        name="mix_ffn",
    )(oa, ob, r, x2, gt1, wpa, wpb, wo, g, sc, sh, gt, wup, wup, cw, cb, wd, e1, e2)


def _group_mods(mod, d, rows_per_seq, tile_rows):
    parts = [mod[:, k * d:(k + 1) * d] for k in range(6)]
    if rows_per_seq >= tile_rows:
        return [p[:, None, :] for p in parts]
    return [jnp.repeat(p, rows_per_seq, axis=0)[None] for p in parts]


def _layer_in(x3, mods, p, *, tm_in, consts):
    nseq, seq, d = x3.shape
    rows = nseq * seq
    tiles_per_group_in = max(seq // tm_in, 1) if seq >= tm_in else rows // tm_in
    sh1, sc1 = mods[0], mods[1]
    kv_t = seq >= tm_in
    k_gain = p['k_gain'].reshape(W_A, 1) if kv_t else p['k_gain']
    return _in_proj(x3.reshape(rows, d), p['g_norm1'], sc1, sh1, p['w_in'], p['wkv_t'], consts['bd'],
                    p['q_gain'], k_gain, tm=tm_in, tiles_per_group=tiles_per_group_in, kv_t=kv_t)


def _layer_out(x3, mods, p, layer, proj, oa, *, s0, conv_prev, tm, kv_t):
    nseq, seq, d = x3.shape
    rows = nseq * seq
    x2 = x3.reshape(rows, d)
    tiles_per_group = max(seq // tm, 1) if seq >= tm else rows // tm
    sh1, sc1, gt1, sh2, sc2, gt2 = mods
    q, k, v, hg, r = proj

    hg4 = hg.reshape(4, nseq, seq, W_B)
    if seq >= HGRN_CHUNK:
        ob3, s_new = _hgrn(hg4, s0, p['lb_logits'], p['hgrn_gain'],
                           layer=layer, chunk=HGRN_CHUNK, valid=HGRN_CHUNK, nseq=1, nchunks=4)
        ob = ob3.reshape(rows, W_B)
    else:
        padded = 8
        hg4 = jnp.pad(hg4, ((0, 0), (0, 0), (0, padded - seq), (0, 0)))
        ob3, s_new = _hgrn(hg4, s0, p['lb_logits'], p['hgrn_gain'],
                           layer=layer, chunk=padded, valid=seq, nseq=4, nchunks=1)
        ob = ob3[:, :seq].reshape(rows, W_B)

    d_ff = p['w_down'].shape[0]
    if seq >= tm:
        e1 = e2 = conv_prev
        ut_rows = 8
    else:
        z = jnp.zeros((nseq, seq - 2, d_ff), F32)
        e1 = jnp.concatenate([conv_prev[:, 1:2], jnp.zeros((nseq, seq - 1, d_ff), F32)], axis=1).reshape(rows, d_ff)
        e2 = jnp.concatenate([conv_prev, z], axis=1).reshape(rows, d_ff)
        ut_rows = tm
    y, ut = _mix_ffn(oa, ob, r, x2, gt1, p['w_pa'], p['w_pb'], p['w_o'],
                     p['g_norm2'], sc2, sh2, gt2, p['w_up'], p['conv_w'], p['conv_b'], p['w_down'], e1, e2,
                     tm=tm, tiles_per_group=tiles_per_group, seq=seq, ut_rows=ut_rows)
    if seq >= tm:
        tps = seq // tm
        conv_new = ut.reshape(nseq, tps, 8, d_ff)[:, tps - 1, 6:8]
    else:
        conv_new = ut.reshape(nseq, seq, d_ff)[:, seq - 2:]
    if kv_t:
        k_out = k.reshape(nseq, H_A, DH_A, seq).transpose(0, 3, 1, 2)
        v_out = v.reshape(nseq, H_A, DH_A, seq).transpose(0, 3, 1, 2)
    else:
        k_out = k.reshape(nseq, seq, H_A, DH_A)
        v_out = v.reshape(nseq, seq, H_A, DH_A)
    return y.reshape(nseq, seq, d), k_out, v_out, s_new, conv_new


def kernel(x_prompt, x_sample, cache_k, cache_v, state_hgrn, state_conv, page_table, c_prompt, c_sample,
           w_ada, b_ada, g_norm1, w_in, q_gain, k_gain, sb_bias, hgrn_lb_logits, hgrn_gain, w_pa, w_pb, w_o,
           g_norm2, w_up, conv_w, conv_b, w_down):
    depth = w_ada.shape[0]
    n_pr, seq, d = x_prompt.shape
    n_dec, dec_seq, _ = x_sample.shape
    d_ff = w_down.shape[1]
    n_pool, page = cache_k.shape[1], cache_k.shape[2]

    mod_all = _ada(jnp.concatenate([c_prompt, c_sample], axis=0), w_ada, b_ada)

    head = np.arange(W_A) // DH_A
    consts = {'bd': jnp.asarray((head[:, None] == head[None, :]).astype(np.float32) / DH_A, dtype=BF16)}
    uo_prompt = _suffix_matrix(ATT_TK)
    uo_page = _suffix_matrix(page)
    ck = jnp.transpose(cache_k, (0, 1, 3, 4, 2)).reshape(depth, n_pool, W_A, page)
    cv = jnp.transpose(cache_v, (0, 1, 3, 4, 2)).reshape(depth, n_pool, W_A, page)
    rowh = np.arange(dec_seq * H_A) % H_A
    qmask = jnp.asarray((rowh[:, None] == head[None, :]), dtype=BF16)

    yp, ys = x_prompt, x_sample
    outs = [[] for _ in range(8)]
    for l in range(depth):
        p = {
            'g_norm1': g_norm1[l][None], 'g_norm2': g_norm2[l][None],
            'w_in': w_in[l].astype(BF16).reshape(d, -1, IN_TN).transpose(1, 0, 2),
            'w_pa': w_pa[l].astype(BF16), 'w_pb': w_pb[l].astype(BF16),
            'wkv_t': w_in[l][:, W_A:3 * W_A].T.reshape(2, W_A, d).astype(BF16),
            'w_o': w_o[l].astype(BF16), 'w_up': w_up[l].astype(BF16), 'w_down': w_down[l].astype(BF16),
            'q_gain': jnp.tile(q_gain[l], H_A)[None], 'k_gain': jnp.tile(k_gain[l], H_A)[None],
            'lb_logits': hgrn_lb_logits, 'hgrn_gain': hgrn_gain[l][None],
            'conv_w': conv_w[l], 'conv_b': conv_b[l][None],
        }

        tm_p, tm_in_p, tm_s = 512, min(1024, seq), n_dec * dec_seq
        mods_p = _group_mods(mod_all[l, :n_pr], d, seq, tm_p)
        mods_s = _group_mods(mod_all[l, n_pr:], d, dec_seq, tm_s)
        proj_p = _layer_in(yp, mods_p, p, tm_in=tm_in_p, consts=consts)
        proj_s = _layer_in(ys, mods_s, p, tm_in=tm_s, consts=consts)

        qs = proj_s[0].reshape(n_dec, dec_seq, 1, W_A)
        qbd = (jnp.broadcast_to(qs, (n_dec, dec_seq, H_A, W_A)).reshape(n_dec, dec_seq * H_A, W_A) * qmask[None])
        bias_rows = jnp.broadcast_to(jnp.tile(sb_bias[l], dec_seq)[:, None], (dec_seq * H_A, V7X_LANES))
        knew = jnp.pad(proj_s[1].reshape(n_dec, dec_seq, W_A), ((0, 0), (0, 8 - dec_seq), (0, 0)))
        vnew = jnp.pad(proj_s[2].reshape(n_dec, dec_seq, W_A), ((0, 0), (0, 8 - dec_seq), (0, 0)))
        oa_p, oa_s = _attention(proj_p[0], proj_p[1], proj_p[2], sb_bias[l], uo_prompt,
                                page_table, qbd, bias_rows, knew, vnew, uo_page, ck, cv,
                                batch=n_pr, seq=seq, layer=l)
        oa_s = oa_s.reshape(n_dec * dec_seq, W_A).astype(BF16)

        yp, kp, vp, hp, cp = _layer_out(
            yp, mods_p, p, l, proj_p, oa_p,
            s0=jnp.zeros((n_pr, H_B, DK_B, DK_B), F32), conv_prev=jnp.zeros((n_pr, CONV_W - 1, d_ff), F32),
            tm=tm_p, kv_t=True)
        ys, ksm, vsm, hs, cs = _layer_out(
            ys, mods_s, p, l, proj_s, oa_s, s0=state_hgrn[l], conv_prev=state_conv[l], tm=tm_s, kv_t=False)
        for lst, val in zip(outs, (kp, vp, ksm, vsm, hp, hs, cp, cs)):
            lst.append(val)

    return (yp, ys) + tuple(jnp.stack(o) for o in outs)
```

```python
import functools

import numpy as np
import jax
import jax.numpy as jnp
from jax import lax
from jax.experimental import pallas as pl
from jax.experimental.pallas import tpu as pltpu

F32 = jnp.float32
BF16 = jnp.bfloat16

EPS = 1e-6
V7X_LANES = 128
V7X_VMEM_LIMIT_BYTES = 56 * 1024 * 1024

H_A = 8
DH_A = 64
W_A = H_A * DH_A
H_B = 4
DK_B = 128
W_B = H_B * DK_B
CONV_W = 3
HGRN_CHUNK = 64

IN_PROJ_TM = 1024
MIX_FFN_TM = 512


def _cparams(sem):
    return pltpu.CompilerParams(dimension_semantics=sem, vmem_limit_bytes=V7X_VMEM_LIMIT_BYTES)


def _sigmoid(x):
    return 1.0 / (1.0 + jnp.exp(-x))


def _dot(a, b):
    return jnp.dot(a, b, preferred_element_type=F32)


def _dot_nt(a, b):
    return lax.dot_general(a, b, (((1,), (1,)), ((), ())), preferred_element_type=F32)


def _split2(x):
    hi = x.astype(BF16)
    lo = (x - hi.astype(F32)).astype(BF16)
    return hi, lo


def _ada_kernel(c_ref, w_ref, b_ref, o_ref):
    c = c_ref[...]
    s = c * _sigmoid(c)
    s_hi, s_lo = _split2(s)
    w_hi, w_lo = _split2(w_ref[...])
    acc = _dot(s_hi, w_hi) + _dot(s_hi, w_lo) + _dot(s_lo, w_hi)
    o_ref[...] = acc + b_ref[...]


def _ada(c_all, w_ada, b_ada):
    depth, d, n6 = w_ada.shape
    nc = c_all.shape[0]
    tn = 1536
    return pl.pallas_call(
        _ada_kernel,
        out_shape=jax.ShapeDtypeStruct((depth, nc, n6), F32),
        grid=(depth, n6 // tn),
        in_specs=[
            pl.BlockSpec((nc, d), lambda l, j: (0, 0)),
            pl.BlockSpec((None, d, tn), lambda l, j: (l, 0, j)),
            pl.BlockSpec((None, 1, tn), lambda l, j: (l, 0, j)),
        ],
        out_specs=pl.BlockSpec((None, nc, tn), lambda l, j: (l, 0, j)),
        compiler_params=_cparams(("parallel", "parallel")),
        name="ada_mod",
    )(c_all, w_ada, b_ada.reshape(depth, 1, n6))


IN_TN = 512


def _in_kernel(x_ref, g_ref, sc_ref, sh_ref, w_ref, wkv_ref, bd_ref, qg_ref, kg_ref,
               q_ref, k_ref, v_ref, hg_ref, r_ref, h_scr, *, kv_t):
    j = pl.program_id(1)

    @pl.when(j == 0)
    def _():
        x = x_ref[...]
        ms = jnp.mean(x * x, axis=-1, keepdims=True)
        y = x * lax.rsqrt(ms + EPS) * g_ref[...]
        h_scr[...] = (y * (1.0 + sc_ref[...]) + sh_ref[...]).astype(BF16)

    def proj():
        return _dot(h_scr[...], w_ref[...])

    def proj_t():
        return _dot_nt(wkv_ref[...], h_scr[...])

    def head_norm(t):
        ms = _dot((t * t).astype(BF16), bd_ref[...])
        return t * lax.rsqrt(ms + EPS)

    @pl.when(j == 0)
    def _():
        q_ref[...] = (head_norm(proj()) * qg_ref[...] * (DH_A ** -0.5)).astype(BF16)

    @pl.when(j == 1)
    def _():
        if kv_t:
            t = proj_t()
            t3 = t.reshape(H_A, DH_A, t.shape[1])
            ms = jnp.mean(t3 * t3, axis=1, keepdims=True)
            k_ref[...] = (t3 * lax.rsqrt(ms + EPS)).reshape(t.shape) * kg_ref[...]
        else:
            t = proj()
            sq_hi, sq_lo = _split2(t * t)
            ms = _dot(sq_hi, bd_ref[...]) + _dot(sq_lo, bd_ref[...])
            k_ref[...] = t * lax.rsqrt(ms + EPS) * kg_ref[...]

    @pl.when(j == 2)
    def _():
        v_ref[...] = proj_t() if kv_t else proj()

    @pl.when(jnp.logical_and(j >= 3, j <= 6))
    def _():
        hg_ref[...] = proj()

    @pl.when(j >= 7)
    def _():
        r_ref[...] = _sigmoid(proj()).astype(r_ref.dtype)


def _in_proj(x2, g, sc, sh, w_tiles, wkv_t, bd, qg, kg, *, tm, tiles_per_group, kv_t):
    rows, d = x2.shape
    nj = w_tiles.shape[0]
    rb = sc.shape[1]
    mod_spec = pl.BlockSpec((None, rb, d), lambda i, j: (i // tiles_per_group, 0, 0))
    kv_sel = lambda j: jnp.clip(j - 1, 0, 1)
    if kv_t:
        nseq = rows // (tiles_per_group * tm)
        kv_shape = (nseq, W_A, tiles_per_group * tm)
        kv_spec = pl.BlockSpec((None, W_A, tm), lambda i, j: (i // tiles_per_group, 0, i % tiles_per_group))
        w_spec = pl.BlockSpec((None, d, IN_TN), lambda i, j: (jnp.where((j == 1) | (j == 2), 0, j), 0, 0))
        kg_spec = pl.BlockSpec((W_A, 1), lambda i, j: (0, 0))
    else:
        kv_shape = (rows, W_A)
        kv_spec = pl.BlockSpec((tm, IN_TN), lambda i, j: (i, 0))
        w_spec = pl.BlockSpec((None, d, IN_TN), lambda i, j: (j, 0, 0))
        kg_spec = pl.BlockSpec((1, W_A), lambda i, j: (0, 0))
    return pl.pallas_call(
        functools.partial(_in_kernel, kv_t=kv_t),
        out_shape=(
            jax.ShapeDtypeStruct((rows, W_A), BF16),
            jax.ShapeDtypeStruct(kv_shape, F32),
            jax.ShapeDtypeStruct(kv_shape, F32),
            jax.ShapeDtypeStruct((4, rows, W_B), F32),
            jax.ShapeDtypeStruct((4, rows, IN_TN), BF16),
        ),
        grid=(rows // tm, nj),
        in_specs=[
            pl.BlockSpec((tm, d), lambda i, j: (i, 0)),
            pl.BlockSpec((1, d), lambda i, j: (0, 0)),
            mod_spec,
            mod_spec,
            w_spec,
            pl.BlockSpec((None, W_A, d), lambda i, j: (kv_sel(j), 0, 0)),
            pl.BlockSpec((W_A, W_A), lambda i, j: (0, 0)),
            pl.BlockSpec((1, W_A), lambda i, j: (0, 0)),
            kg_spec,
        ],
        out_specs=(
            pl.BlockSpec((tm, IN_TN), lambda i, j: (i, 0)),
            kv_spec,
            kv_spec,
            pl.BlockSpec((None, tm, IN_TN), lambda i, j: (jnp.clip(j - 3, 0, 3), i, 0)),
            pl.BlockSpec((None, tm, IN_TN), lambda i, j: (jnp.clip(j - 7, 0, 3), i, 0)),
        ),
        scratch_shapes=[pltpu.VMEM((tm, d), BF16)],
        compiler_params=_cparams(("parallel", "arbitrary")),
        name="in_proj",
    )(x2, g, sc, sh, w_tiles, wkv_t, bd, qg, kg)


def _sb_keep(z, mask):
    sp = jnp.maximum(z, 0.0) + jnp.log(1.0 + jnp.exp(-jnp.abs(z)))
    spm = sp if mask is None else jnp.where(mask, sp, 0.0)
    return spm, z - sp


def _sb_weights(log_beta, after, c, mask):
    n = log_beta.shape[1]
    c_full = c if n == V7X_LANES else jnp.concatenate([c] * (n // V7X_LANES), axis=1)
    a = jnp.exp(log_beta + after + c_full)
    return a if mask is None else jnp.where(mask, a, 0.0)


def _sb_multi(problems):
    keeps = [[_sb_keep(z, m) for z, m in zip(zs, masks)] for zs, _, _, masks in problems]
    afters = [[_dot(sp.astype(BF16), neg_u) for sp, _ in kp] for kp, (_, _, neg_u, _) in zip(keeps, problems)]
    tots = [[jnp.sum(sp, axis=1, keepdims=True) for sp, _ in kp] for kp in keeps]
    out = []
    for kp, af, tt, (_, c, _, masks) in zip(keeps, afters, tots, problems):
        ws = []
        for (_, log_beta), after, tot, m in zip(kp, af, tt, masks):
            ws.append(_sb_weights(log_beta, after, c, m).astype(BF16))
            c = c - tot
        out.append((ws, c))
    return out


def _sb_blocks(zs, c, neg_u, masks):
    return _sb_multi([(zs, c, neg_u, masks)])[0]


def _suffix_matrix(n):
    j = np.arange(n)[:, None]
    s = np.arange(n)[None, :]
    return jnp.asarray(-(j > s).astype(np.float32), dtype=BF16)


ATT_TQ = 256
ATT_TK = 256


def _attn_kernel(pt_ref, bias_ref, q_ref, kt_ref, vt_ref, u_ref, qbd_ref, sbias_ref, knew_ref, vnew_ref, up_ref,
                 *rest, npg):
    k_refs = rest[:npg]
    v_refs = rest[npg:2 * npg]
    o_ref, os_ref, kv_scr, acc_scr, c_scr, sacc_scr, sc_scr = rest[2 * npg:]
    hp = pl.program_id(1)
    qi = pl.program_id(2)
    tq, tk = ATT_TQ, ATT_TK
    nkb = kv_scr.shape[1]
    first_head = lax.broadcasted_iota(jnp.int32, (1, V7X_LANES), 1) < DH_A
    nrow = qbd_ref.shape[0]
    dec_seq = nrow // H_A
    page = up_ref.shape[0]
    qbd = qbd_ref[...]
    sbias = sbias_ref[...]
    up = up_ref[...]

    @pl.when(qi == 0)
    def _():
        for kb in range(nkb):
            kv_scr[0, kb] = kt_ref[:, kb * tk:(kb + 1) * tk].astype(BF16)
            kv_scr[1, kb] = vt_ref[:, kb * tk:(kb + 1) * tk].astype(BF16)
        pad = jnp.zeros((page - knew_ref.shape[0], W_A), F32)
        kn = jnp.concatenate([knew_ref[...], pad], axis=0).astype(BF16)
        vn = jnp.concatenate([vnew_ref[...], pad], axis=0).astype(BF16)
        t = lax.broadcasted_iota(jnp.int32, (nrow, page), 0) >> 3
        s = lax.broadcasted_iota(jnp.int32, (nrow, page), 1)
        ws, c = _sb_blocks([_dot_nt(qbd, kn) + sbias], jnp.zeros((nrow, V7X_LANES), F32), up, [s < t])
        sacc_scr[...] = _dot(ws[0], vn)
        sc_scr[...] = c

    q = q_ref[...]
    zero = jnp.zeros_like(q)
    q2 = jnp.concatenate([jnp.where(first_head, q, zero), jnp.where(first_head, zero, q)], axis=0)
    u = u_ref[...]
    b0 = bias_ref[2 * hp]
    b1 = bias_ref[2 * hp + 1]
    row = lax.broadcasted_iota(jnp.int32, (2 * tq, tk), 0) & (tq - 1)
    col = lax.broadcasted_iota(jnp.int32, (2 * tq, tk), 1)

    def logits(kb):
        s = _dot(q2, kv_scr[0, kb])
        return jnp.concatenate([s[:tq] + b0, s[tq:] + b1], axis=0)

    def steps(kbs, masks):
        ws, c = _sb_blocks([logits(kb) for kb in kbs], c_scr[...], u, masks)
        acc = acc_scr[...]
        for kb, w in zip(kbs, ws):
            acc = acc + _dot_nt(w, kv_scr[1, kb])
        acc_scr[...] = acc
        c_scr[...] = c

    acc_scr[...] = jnp.zeros_like(acc_scr)
    c_scr[...] = jnp.zeros_like(c_scr)
    steps([qi], [col < row])

    zs = [_dot(qbd, k_refs[i][...].astype(BF16)) + sbias for i in range(npg)]
    ws, c = _sb_blocks(zs, sc_scr[...], up, [None] * npg)
    sacc = sacc_scr[...]
    for i in range(npg):
        sacc = sacc + _dot_nt(ws[i], v_refs[i][...].astype(BF16))
    sacc_scr[...] = sacc
    sc_scr[...] = c

    def body(i, carry):
        kb = qi - 1 - 4 * i
        steps([kb, kb - 1, kb - 2, kb - 3], [None] * 4)
        return carry

    lax.fori_loop(0, qi // 4, body, 0)
    rem = qi & 3

    @pl.when((rem & 2) != 0)
    def _():
        steps([rem - 1, rem - 2], [None] * 2)

    @pl.when((rem & 1) != 0)
    def _():
        steps([0], [None])

    o_ref[...] = jnp.where(first_head, acc_scr[:tq], acc_scr[tq:]).astype(o_ref.dtype)

    @pl.when(qi == pl.num_programs(2) - 1)
    def _():
        r = lax.broadcasted_iota(jnp.int32, (nrow, W_A), 0)
        l = lax.broadcasted_iota(jnp.int32, (nrow, W_A), 1)
        own = (r & (H_A - 1)) == (l >> 6)
        om = jnp.where(own, sacc_scr[...], 0.0)
        os_ref[...] = jnp.sum(om.reshape(dec_seq, H_A, W_A), axis=1)


def _attention(q, kt, vt, sb_bias, u, page_table, qbd, bias_rows, knew, vnew, u_page, cache_k, cache_v,
               *, batch, seq, layer):
    rows = q.shape[0]
    nq = seq // ATT_TQ
    npair = H_A // 2
    nb, nrow, _ = qbd.shape
    n_pages = page_table.shape[1]
    page = cache_k.shape[3]
    dec_seq = nrow // H_A
    assert nb == batch * npair and n_pages % nq == 0
    npg = n_pages // nq

    def page_spec(i):
        def imap(b, hp, qi, pt):
            return (layer, pt[b * npair + hp, n_pages - 1 - (qi * npg + i)], 0, 0)
        return pl.BlockSpec((None, None, W_A, page), imap)

    sample_seq = lambda b, hp, qi, pt: (b * npair + hp, 0, 0)
    grid_spec = pltpu.PrefetchScalarGridSpec(
        num_scalar_prefetch=1,
        grid=(batch, npair, nq),
        in_specs=[
            pl.BlockSpec(memory_space=pltpu.SMEM),
            pl.BlockSpec((ATT_TQ, V7X_LANES), lambda b, hp, qi, pt: (b * nq + qi, hp)),
            pl.BlockSpec((None, V7X_LANES, seq), lambda b, hp, qi, pt: (b, hp, 0)),
            pl.BlockSpec((None, V7X_LANES, seq), lambda b, hp, qi, pt: (b, hp, 0)),
            pl.BlockSpec((ATT_TK, ATT_TK), lambda b, hp, qi, pt: (0, 0)),
            pl.BlockSpec((None, nrow, W_A), sample_seq),
            pl.BlockSpec((nrow, V7X_LANES), lambda b, hp, qi, pt: (0, 0)),
            pl.BlockSpec((None, 8, W_A), sample_seq),
            pl.BlockSpec((None, 8, W_A), sample_seq),
            pl.BlockSpec((page, page), lambda b, hp, qi, pt: (0, 0)),
        ] + [page_spec(i) for i in range(npg)] + [page_spec(i) for i in range(npg)],
        out_specs=(
            pl.BlockSpec((ATT_TQ, V7X_LANES), lambda b, hp, qi, pt: (b * nq + qi, hp)),
            pl.BlockSpec((None, dec_seq, W_A), sample_seq),
        ),
        scratch_shapes=[
            pltpu.VMEM((2, seq // ATT_TK, V7X_LANES, ATT_TK), BF16),
            pltpu.VMEM((2 * ATT_TQ, V7X_LANES), F32),
            pltpu.VMEM((2 * ATT_TQ, V7X_LANES), F32),
            pltpu.VMEM((nrow, W_A), F32),
            pltpu.VMEM((nrow, V7X_LANES), F32),
        ],
    )
    return pl.pallas_call(
        functools.partial(_attn_kernel, npg=npg),
        out_shape=(
            jax.ShapeDtypeStruct((rows, W_A), BF16),
            jax.ShapeDtypeStruct((nb, dec_seq, W_A), F32),
        ),
        grid_spec=grid_spec,
        compiler_params=_cparams(("parallel", "parallel", "arbitrary")),
        name="attention",
    )(page_table, sb_bias, q, kt, vt, u, qbd, bias_rows, knew, vnew, u_page,
      *([cache_k] * npg), *([cache_v] * npg))


def _seg_bcast(x, n, off):
    rows, lanes = x.shape
    if n >= 8:
        parts = [jnp.broadcast_to(x[b * n + off:b * n + off + 1, :], (n, lanes)) for b in range(rows // n)]
        return jnp.concatenate(parts, axis=0)
    sub = lax.broadcasted_iota(jnp.int32, (8, lanes), 0)
    parts = []
    for g in range(rows // 8):
        acc = jnp.broadcast_to(x[g * 8 + off:g * 8 + off + 1, :], (8, lanes))
        for sb in range(1, 8 // n):
            r = g * 8 + sb * n + off
            acc = jnp.where(sub >= sb * n, jnp.broadcast_to(x[r:r + 1, :], (8, lanes)), acc)
        parts.append(acc)
    return jnp.concatenate(parts, axis=0)


def _hgrn_levels(chunk):
    out = []
    n = chunk
    while n >= 2:
        out.append(n)
        n //= 2
    return out


def _hgrn_masks(chunk, groups):
    rows = chunk * groups
    t = np.arange(rows)
    m = [(t[:, None] // n == t[None, :] // n) & (t[:, None] % n >= n // 2) & (t[None, :] % n < n // 2)
         for n in _hgrn_levels(chunk)]
    m.append(t[:, None] == t[None, :])
    return jnp.asarray(np.stack(m).astype(np.float32))


def _hgrn_kernel(lbl_ref, gain_ref, masks_ref, tri_ref, hg_ref, s0_ref, ob_ref, sout_ref, s_scr,
                 *, layer, chunk, valid, nseq, nchunks):
    j = pl.program_id(1)
    groups = nseq * H_B
    rows = groups * chunk
    levels = _hgrn_levels(chunk)

    @pl.when(j == 0)
    def _():
        s_scr[...] = s0_ref[...].reshape(groups, DK_B, DK_B)

    lg = lbl_ref[...]
    ex = jnp.exp(lg - jnp.max(lg, axis=0, keepdims=True))
    wts = ex / jnp.sum(ex, axis=0, keepdims=True)
    lb_row = jnp.sum(wts[:layer + 1], axis=0, keepdims=True) - wts[0:1]

    def stack(get):
        return jnp.concatenate([get(s, h) for s in range(nseq) for h in range(H_B)], axis=0)

    lbs = stack(lambda s, h: jnp.broadcast_to(lb_row[:, h * DK_B:(h + 1) * DK_B], (chunk, DK_B)))
    tloc = lax.broadcasted_iota(jnp.int32, (rows, DK_B), 0) & (chunk - 1)
    tri = tri_ref[...]
    gain = gain_ref[...]

    for ci in range(nchunks):
        r0 = ci * chunk

        def seg(col, r0=r0):
            return stack(lambda s, h: hg_ref[col, s, r0:r0 + chunk, h * DK_B:(h + 1) * DK_B])

        zq, zf, vi, zg = seg(0), seg(1), seg(2), seg(3)
        logf = jnp.log(lbs + (1.0 - lbs) * _sigmoid(zf))
        kk = (1.0 - lbs) * _sigmoid(-zf)
        qs = zq * _sigmoid(zq)
        if valid < chunk:
            ok = tloc < valid
            logf = jnp.where(ok, logf, 0.0)
            kk = jnp.where(ok, kk, 0.0)
            qs = jnp.where(ok, qs, 0.0)

        l_hi = logf.astype(BF16)
        rem = logf - l_hi.astype(F32)
        l_mid = rem.astype(BF16)
        l_lo = (rem - l_mid.astype(F32)).astype(BF16)
        cum = _dot(tri, l_hi) + _dot(tri, l_mid) + _dot(tri, l_lo)

        att = _dot_nt(qs.astype(BF16), kk.astype(BF16)) * masks_ref[len(levels)]
        for li, n in enumerate(levels):
            ref_row = _seg_bcast(cum, n, n // 2 - 1)
            e = jnp.exp(-jnp.abs(cum - ref_row))
            att = att + _dot_nt((qs * e).astype(BF16), (kk * e).astype(BF16)) * masks_ref[li]

        o = _dot(att.astype(BF16), vi.astype(BF16))
        qe = qs * jnp.exp(cum)
        o = o + jnp.concatenate(
            [_dot(qe[g * chunk:(g + 1) * chunk].astype(BF16), s_scr[g].astype(BF16)) for g in range(groups)],
            axis=0)

        on = o * lax.rsqrt(jnp.mean(o * o, axis=-1, keepdims=True) + EPS) * gain
        og = on * (zg * _sigmoid(zg))
        for s in range(nseq):
            ob_ref[s, r0:r0 + chunk, :] = jnp.concatenate(
                [og[(s * H_B + h) * chunk:(s * H_B + h + 1) * chunk] for h in range(H_B)], axis=1)

        last = _seg_bcast(cum, chunk, chunk - 1)
        kd_t = (kk * jnp.exp(last - cum)).T
        dec_t = jnp.exp(last).T
        colg = lax.broadcasted_iota(jnp.int32, (DK_B, rows), 1) >> (chunk.bit_length() - 1)
        vb = vi.astype(BF16)
        for g in range(groups):
            kd_g = jnp.where(colg == g, kd_t, 0.0).astype(BF16)
            s_scr[g] = dec_t[:, g * chunk:g * chunk + 1] * s_scr[g] + _dot(kd_g, vb)

    @pl.when(j == pl.num_programs(1) - 1)
    def _():
        sout_ref[...] = s_scr[...].reshape(nseq, H_B, DK_B, DK_B)


def _hgrn(hg4, s0, lb_logits, gain, *, layer, chunk, valid, nseq, nchunks):
    _, nb, tp, _ = hg4.shape
    rb = chunk * nchunks
    groups = nseq * H_B
    rows = groups * chunk
    masks = _hgrn_masks(chunk, groups)
    t = np.arange(rows)
    tri = jnp.asarray(((t[:, None] // chunk == t[None, :] // chunk) & (t[None, :] <= t[:, None])).astype(np.float32),
                      dtype=BF16)
    kern = functools.partial(_hgrn_kernel, layer=layer, chunk=chunk, valid=valid, nseq=nseq, nchunks=nchunks)
    return pl.pallas_call(
        kern,
        out_shape=(
            jax.ShapeDtypeStruct((nb, tp, W_B), F32),
            jax.ShapeDtypeStruct((nb, H_B, DK_B, DK_B), F32),
        ),
        grid=(nb // nseq, tp // rb),
        in_specs=[
            pl.BlockSpec(lb_logits.shape, lambda b, j: (0, 0)),
            pl.BlockSpec((1, DK_B), lambda b, j: (0, 0)),
            pl.BlockSpec(masks.shape, lambda b, j: (0, 0, 0)),
            pl.BlockSpec((rows, rows), lambda b, j: (0, 0)),
            pl.BlockSpec((4, nseq, rb, W_B), lambda b, j: (0, b, j, 0)),
            pl.BlockSpec((nseq, H_B, DK_B, DK_B), lambda b, j: (b, 0, 0, 0)),
        ],
        out_specs=(
            pl.BlockSpec((nseq, rb, W_B), lambda b, j: (b, j, 0)),
            pl.BlockSpec((nseq, H_B, DK_B, DK_B), lambda b, j: (b, 0, 0, 0)),
        ),
        scratch_shapes=[pltpu.VMEM((groups, DK_B, DK_B), F32)],
        compiler_params=_cparams(("parallel", "arbitrary")),
        name="hgrn",
    )(lb_logits, gain, masks, tri, hg4, s0)


def _mix(oa_ref, ob_ref, r_ref, x_ref, gt_ref, wpa_ref, wpb_ref, wo_ref):
    half = r_ref.shape[2]
    pa = _dot(oa_ref[...], wpa_ref[...])
    pb = _dot(ob_ref[...].astype(BF16), wpb_ref[...])
    merged = jnp.concatenate(
        [r_ref[c].astype(F32) * pa[:, c * half:(c + 1) * half]
         + r_ref[2 + c].astype(F32) * pb[:, c * half:(c + 1) * half] for c in range(2)], axis=1)
    y = _dot(merged.astype(BF16), wo_ref[...])
    return x_ref[...] + gt_ref[...] * y


FFN_CHUNKS = 2
FFN_SUB = 256


def _gelu_tanh(x):
    return 0.5 * x * (1.0 + jnp.tanh(0.7978845608028654 * (x + 0.044715 * x * x * x)))


def _ffn_kernel(oa_ref, ob_ref, r_ref, x_ref, gt1_ref, wpa_ref, wpb_ref, wo_ref,
                g_ref, sc_ref, sh_ref, gt_ref, wu_ref, wv_ref, cw_ref, cb_ref, wd_ref, e1_ref, e2_ref,
                out_ref, ut_ref, x1_scr, h_scr, acc_scr, carry_scr, *, seq, tiles_per_seq):
    i = pl.program_id(0)
    c = pl.program_id(1)
    tm = x_ref.shape[0]
    tc = wu_ref.shape[1]

    @pl.when(c == 0)
    def _():
        x = _mix(oa_ref, ob_ref, r_ref, x_ref, gt1_ref, wpa_ref, wpb_ref, wo_ref)
        x1_scr[...] = x
        ms = jnp.mean(x * x, axis=-1, keepdims=True)
        y = x * lax.rsqrt(ms + EPS) * g_ref[...]
        h_scr[...] = (y * (1.0 + sc_ref[...]) + sh_ref[...]).astype(BF16)
        acc_scr[...] = jnp.zeros_like(acc_scr)

    h = h_scr[...]
    col0 = pl.multiple_of(c * tc, tc)
    if tiles_per_seq >= 1:
        @pl.when((i % tiles_per_seq) == 0)
        def _():
            carry_scr[:, pl.ds(col0, tc)] = e1_ref[...]

    acts = []
    for a in range(0, tc, FFN_SUB):
        b = min(a + FFN_SUB, tc)
        w = b - a
        u = _dot(h, wu_ref[:, a:b])
        v = _dot(h, wv_ref[:, a:b])
        row = lax.broadcasted_iota(jnp.int32, (tm, w), 0)
        if tiles_per_seq >= 1:
            cols = pl.ds(pl.multiple_of(col0 + a, V7X_LANES), w)
            prev = carry_scr[:, cols]
            m1 = jnp.where(row >= 1, pltpu.roll(u, 1, 0), jnp.broadcast_to(prev[1:2], (tm, w)))
            m2 = jnp.where(row >= 2, pltpu.roll(u, 2, 0),
                           jnp.where(row == 1, jnp.broadcast_to(prev[1:2], (tm, w)),
                                     jnp.broadcast_to(prev[0:1], (tm, w))))
            carry_scr[:, cols] = u[tm - 2:tm, :]
        else:
            t = row & (seq - 1)
            m1 = jnp.where(t >= 1, pltpu.roll(u, 1, 0), e1_ref[:, a:b])
            m2 = jnp.where(t >= 2, pltpu.roll(u, 2, 0), e2_ref[:, a:b])
        conv = cb_ref[:, a:b] + m2 * cw_ref[0:1, a:b] + m1 * cw_ref[1:2, a:b] + u * cw_ref[2:3, a:b]
        acts.append((_gelu_tanh(conv) * v).astype(BF16))
        ut_ref[:, a:b] = u[tm - ut_ref.shape[0]:, :]
    acc_scr[...] = acc_scr[...] + _dot(jnp.concatenate(acts, axis=1), wd_ref[...])

    @pl.when(c == pl.num_programs(1) - 1)
    def _():
        out_ref[...] = x1_scr[...] + gt_ref[...] * acc_scr[...]


def _mix_ffn(oa, ob, r, x2, gt1, wpa, wpb, wo, g, sc, sh, gt, wup, cw, cb, wd, e1, e2,
             *, tm, tiles_per_group, seq, ut_rows):
    rows, d = x2.shape
    d_ff = wd.shape[0]
    nc = FFN_CHUNKS
    tc = d_ff // nc
    assert tc * nc == d_ff and tc % V7X_LANES == 0
    rb = sc.shape[1]
    tiles_per_seq = seq // tm
    mod_spec = pl.BlockSpec((None, rb, d), lambda i, c: (i // tiles_per_group, 0, 0))
    if tiles_per_seq >= 1:
        e_spec = pl.BlockSpec((None, 2, tc), lambda i, c: (i // tiles_per_seq, 0, c))
    else:
        e_spec = pl.BlockSpec((tm, tc), lambda i, c: (i, c))
    kern = functools.partial(_ffn_kernel, seq=seq, tiles_per_seq=tiles_per_seq)
    return pl.pallas_call(
        kern,
        out_shape=(
            jax.ShapeDtypeStruct((rows, d), F32),
            jax.ShapeDtypeStruct((rows // tm, ut_rows, d_ff), F32),
        ),
        grid=(rows // tm, nc),
        in_specs=[
            pl.BlockSpec((tm, W_A), lambda i, c: (i, 0)),
            pl.BlockSpec((tm, W_B), lambda i, c: (i, 0)),
            pl.BlockSpec((4, tm, IN_TN), lambda i, c: (0, i, 0)),
            pl.BlockSpec((tm, d), lambda i, c: (i, 0)),
            mod_spec,
            pl.BlockSpec((W_A, d), lambda i, c: (0, 0)),
            pl.BlockSpec((W_B, d), lambda i, c: (0, 0)),
            pl.BlockSpec((d, d), lambda i, c: (0, 0)),
            pl.BlockSpec((1, d), lambda i, c: (0, 0)),
            mod_spec, mod_spec, mod_spec,
            pl.BlockSpec((d, tc), lambda i, c: (0, c)),
            pl.BlockSpec((d, tc), lambda i, c: (0, nc + c)),
            pl.BlockSpec((CONV_W, tc), lambda i, c: (0, c)),
            pl.BlockSpec((1, tc), lambda i, c: (0, c)),
            pl.BlockSpec((tc, d), lambda i, c: (c, 0)),
            e_spec, e_spec,
        ],
        out_specs=(
            pl.BlockSpec((tm, d), lambda i, c: (i, 0)),
            pl.BlockSpec((None, ut_rows, tc), lambda i, c: (i, 0, c)),
        ),
        scratch_shapes=[
            pltpu.VMEM((tm, d), F32),
            pltpu.VMEM((tm, d), BF16),
            pltpu.VMEM((tm, d), F32),
            pltpu.VMEM((2, d_ff), F32),
        ],
        compiler_params=_cparams(("arbitrary", "arbitrary")),
        name="mix_ffn",
    )(oa, ob, r, x2, gt1, wpa, wpb, wo, g, sc, sh, gt, wup, wup, cw, cb, wd, e1, e2)


def _group_mods(mod, d, rows_per_seq, tile_rows):
    parts = [mod[:, k * d:(k + 1) * d] for k in range(6)]
    if rows_per_seq >= tile_rows:
        return [p[:, None, :] for p in parts]
    return [jnp.repeat(p, rows_per_seq, axis=0)[None] for p in parts]


def _layer_in(x3, mods, p, *, tm_in, consts):
    nseq, seq, d = x3.shape
    rows = nseq * seq
    tiles_per_group_in = max(seq // tm_in, 1) if seq >= tm_in else rows // tm_in
    sh1, sc1 = mods[0], mods[1]
    kv_t = seq >= tm_in
    k_gain = p['k_gain'].reshape(W_A, 1) if kv_t else p['k_gain']
    return _in_proj(x3.reshape(rows, d), p['g_norm1'], sc1, sh1, p['w_in'], p['wkv_t'], consts['bd'],
                    p['q_gain'], k_gain, tm=tm_in, tiles_per_group=tiles_per_group_in, kv_t=kv_t)


def _layer_out(x3, mods, p, layer, proj, oa, *, s0, conv_prev, tm, kv_t):
    nseq, seq, d = x3.shape
    rows = nseq * seq
    x2 = x3.reshape(rows, d)
    tiles_per_group = max(seq // tm, 1) if seq >= tm else rows // tm
    sh1, sc1, gt1, sh2, sc2, gt2 = mods
    q, k, v, hg, r = proj

    hg4 = hg.reshape(4, nseq, seq, W_B)
    if seq >= HGRN_CHUNK:
        ob3, s_new = _hgrn(hg4, s0, p['lb_logits'], p['hgrn_gain'],
                           layer=layer, chunk=HGRN_CHUNK, valid=HGRN_CHUNK, nseq=1, nchunks=4)
        ob = ob3.reshape(rows, W_B)
    else:
        padded = 8
        hg4 = jnp.pad(hg4, ((0, 0), (0, 0), (0, padded - seq), (0, 0)))
        ob3, s_new = _hgrn(hg4, s0, p['lb_logits'], p['hgrn_gain'],
                           layer=layer, chunk=padded, valid=seq, nseq=4, nchunks=1)
        ob = ob3[:, :seq].reshape(rows, W_B)

    d_ff = p['w_down'].shape[0]
    if seq >= tm:
        e1 = e2 = conv_prev
        ut_rows = 8
    else:
        z = jnp.zeros((nseq, seq - 2, d_ff), F32)
        e1 = jnp.concatenate([conv_prev[:, 1:2], jnp.zeros((nseq, seq - 1, d_ff), F32)], axis=1).reshape(rows, d_ff)
        e2 = jnp.concatenate([conv_prev, z], axis=1).reshape(rows, d_ff)
        ut_rows = tm
    y, ut = _mix_ffn(oa, ob, r, x2, gt1, p['w_pa'], p['w_pb'], p['w_o'],
                     p['g_norm2'], sc2, sh2, gt2, p['w_up'], p['conv_w'], p['conv_b'], p['w_down'], e1, e2,
                     tm=tm, tiles_per_group=tiles_per_group, seq=seq, ut_rows=ut_rows)
    if seq >= tm:
        tps = seq // tm
        conv_new = ut.reshape(nseq, tps, 8, d_ff)[:, tps - 1, 6:8]
    else:
        conv_new = ut.reshape(nseq, seq, d_ff)[:, seq - 2:]
    if kv_t:
        k_out = k.reshape(nseq, H_A, DH_A, seq).transpose(0, 3, 1, 2)
        v_out = v.reshape(nseq, H_A, DH_A, seq).transpose(0, 3, 1, 2)
    else:
        k_out = k.reshape(nseq, seq, H_A, DH_A)
        v_out = v.reshape(nseq, seq, H_A, DH_A)
    return y.reshape(nseq, seq, d), k_out, v_out, s_new, conv_new


def kernel(x_prompt, x_sample, cache_k, cache_v, state_hgrn, state_conv, page_table, c_prompt, c_sample,
           w_ada, b_ada, g_norm1, w_in, q_gain, k_gain, sb_bias, hgrn_lb_logits, hgrn_gain, w_pa, w_pb, w_o,
           g_norm2, w_up, conv_w, conv_b, w_down):
    depth = w_ada.shape[0]
    n_pr, seq, d = x_prompt.shape
    n_dec, dec_seq, _ = x_sample.shape
    d_ff = w_down.shape[1]
    n_pool, page = cache_k.shape[1], cache_k.shape[2]

    mod_all = _ada(jnp.concatenate([c_prompt, c_sample], axis=0), w_ada, b_ada)

    head = np.arange(W_A) // DH_A
    consts = {'bd': jnp.asarray((head[:, None] == head[None, :]).astype(np.float32) / DH_A, dtype=BF16)}
    uo_prompt = _suffix_matrix(ATT_TK)
    uo_page = _suffix_matrix(page)
    ck = jnp.transpose(cache_k, (0, 1, 3, 4, 2)).reshape(depth, n_pool, W_A, page)
    cv = jnp.transpose(cache_v, (0, 1, 3, 4, 2)).reshape(depth, n_pool, W_A, page)
    rowh = np.arange(dec_seq * H_A) % H_A
    qmask = jnp.asarray((rowh[:, None] == head[None, :]), dtype=BF16)

    yp, ys = x_prompt, x_sample
    outs = [[] for _ in range(8)]
    for l in range(depth):
        p = {
            'g_norm1': g_norm1[l][None], 'g_norm2': g_norm2[l][None],
            'w_in': w_in[l].astype(BF16).reshape(d, -1, IN_TN).transpose(1, 0, 2),
            'w_pa': w_pa[l].astype(BF16), 'w_pb': w_pb[l].astype(BF16),
            'wkv_t': w_in[l][:, W_A:3 * W_A].T.reshape(2, W_A, d).astype(BF16),
            'w_o': w_o[l].astype(BF16), 'w_up': w_up[l].astype(BF16), 'w_down': w_down[l].astype(BF16),
            'q_gain': jnp.tile(q_gain[l], H_A)[None], 'k_gain': jnp.tile(k_gain[l], H_A)[None],
            'lb_logits': hgrn_lb_logits, 'hgrn_gain': hgrn_gain[l][None],
            'conv_w': conv_w[l], 'conv_b': conv_b[l][None],
        }

        tm_p, tm_in_p, tm_s = min(MIX_FFN_TM, seq), min(IN_PROJ_TM, seq), n_dec * dec_seq
        mods_p = _group_mods(mod_all[l, :n_pr], d, seq, tm_p)
        mods_s = _group_mods(mod_all[l, n_pr:], d, dec_seq, tm_s)
        proj_p = _layer_in(yp, mods_p, p, tm_in=tm_in_p, consts=consts)
        proj_s = _layer_in(ys, mods_s, p, tm_in=tm_s, consts=consts)

        qs = proj_s[0].reshape(n_dec, dec_seq, 1, W_A)
        qbd = (jnp.broadcast_to(qs, (n_dec, dec_seq, H_A, W_A)).reshape(n_dec, dec_seq * H_A, W_A) * qmask[None])
        bias_rows = jnp.broadcast_to(jnp.tile(sb_bias[l], dec_seq)[:, None], (dec_seq * H_A, V7X_LANES))
        knew = jnp.pad(proj_s[1].reshape(n_dec, dec_seq, W_A), ((0, 0), (0, 8 - dec_seq), (0, 0)))
        vnew = jnp.pad(proj_s[2].reshape(n_dec, dec_seq, W_A), ((0, 0), (0, 8 - dec_seq), (0, 0)))
        oa_p, oa_s = _attention(proj_p[0], proj_p[1], proj_p[2], sb_bias[l], uo_prompt,
                                page_table, qbd, bias_rows, knew, vnew, uo_page, ck, cv,
                                batch=n_pr, seq=seq, layer=l)
        oa_s = oa_s.reshape(n_dec * dec_seq, W_A).astype(BF16)

        yp, kp, vp, hp, cp = _layer_out(
            yp, mods_p, p, l, proj_p, oa_p,
            s0=jnp.zeros((n_pr, H_B, DK_B, DK_B), F32), conv_prev=jnp.zeros((n_pr, CONV_W - 1, d_ff), F32),
            tm=tm_p, kv_t=True)
        ys, ksm, vsm, hs, cs = _layer_out(
            ys, mods_s, p, l, proj_s, oa_s, s0=state_hgrn[l], conv_prev=state_conv[l], tm=tm_s, kv_t=False)
        for lst, val in zip(outs, (kp, vp, ksm, vsm, hp, hs, cp, cs)):
            lst.append(val)

    return (yp, ys) + tuple(jnp.stack(o) for o in outs)
```
